```python
import numpy as np
import jax
import jax.numpy as jnp
from jax import lax

D_MODEL = 2048
BATCH = 4
SEQ = 4096
DEPTH = 2

GRID_W = 64
CTX_LEN = 256
Q_BLOCK = 128
ROPE_THETA = 10000.0
EPS = 1e-6

GQA_HEADS = 8
GQA_KV_HEADS = 2
GQA_GROUP = GQA_HEADS // GQA_KV_HEADS
HEAD_DIM = 128
CONV_DIM = 1024
CONV_WIDTH = 3
MLA_HEADS = 8
MLA_Q_LORA = 512
MLA_KV_LORA = 256
MLA_NOPE = 128
MLA_ROPE = 64
MLA_V = 128
MLA_QK = MLA_NOPE + MLA_ROPE
N_BRANCH = 3
COL_SIZES = (GQA_KV_HEADS * HEAD_DIM, GQA_KV_HEADS * HEAD_DIM, MLA_KV_LORA, MLA_ROPE,
             GQA_HEADS * HEAD_DIM, MLA_Q_LORA,
             CONV_DIM, CONV_DIM, CONV_DIM, N_BRANCH * D_MODEL)
KV_COLS = sum(COL_SIZES[:4])
REST_OFF = sum(COL_SIZES[:6])
IN_COLS = sum(COL_SIZES)
N_GROUPS = 4
EXPERTS_PER_GROUP = 8
N_EXPERTS = N_GROUPS * EXPERTS_PER_GROUP
TOP_K = 2
D_EXPERT = 512
MOE_BLOCK = 256

kernel_name = 'hybrid_gqa_conv_mla_hmoe_dit'


def rms_norm(x, g):
    xf = x.astype(jnp.float32)
    y = xf * lax.rsqrt(jnp.mean(xf * xf, axis=-1, keepdims=True) + EPS)
    return (y * g.astype(jnp.float32)).astype(x.dtype)


def split_cols(p, sizes):
    offs = np.cumsum((0,) + tuple(sizes))
    return [p[..., int(a):int(b)] for a, b in zip(offs[:-1], offs[1:])]


def rope_tables(rows, cols, rot_dim):
    n = rot_dim // 4
    inv = ROPE_THETA ** (-jnp.arange(n, dtype=jnp.float32) / n)
    ang = jnp.concatenate([rows[:, None] * inv, cols[:, None] * inv], axis=-1)
    return jnp.cos(ang), jnp.sin(ang)


def apply_rope(x, cos, sin):
    shape = (1, x.shape[1]) + (1,) * (x.ndim - 3) + (cos.shape[-1],)
    cs = cos.reshape(shape).astype(x.dtype)
    sn = sin.reshape(shape).astype(x.dtype)
    half = x.shape[-1] // 2
    x1, x2 = x[..., :half], x[..., half:]
    return jnp.concatenate([x1 * cs - x2 * sn, x2 * cs + x1 * sn], axis=-1)


def rope_tail(x, cos, sin):
    return jnp.concatenate([x[..., :MLA_NOPE], apply_rope(x[..., MLA_NOPE:], cos, sin)], axis=-1)


def blocked_attention(q, k, v):
    b, sq, hk, g, dk = q.shape
    nb = sq // Q_BLOCK
    scale = dk ** -0.5
    qb = jnp.moveaxis(q.reshape(b, nb, Q_BLOCK, hk, g, dk), 1, 0)

    def one_block(qi):
        s = jnp.einsum('bqhgd,bkhd->bhgqk', qi, k, preferred_element_type=jnp.float32) * scale
        pr = jax.nn.softmax(s, axis=-1).astype(v.dtype)
        return jnp.einsum('bhgqk,bkhd->bqhgd', pr, v)

    o = lax.map(one_block, qb)
    return jnp.moveaxis(o, 0, 1).reshape(b, sq, hk * g * v.shape[-1])


def attn_keys(p_kv, gqa_k_norm, mla_kv_lora_norm, w_ukv, mla_k_norm):
    ka, va, ckv, krope = split_cols(p_kv, COL_SIZES[:4])
    lead = p_kv.shape[:-1]
    ka = rms_norm(ka.reshape(lead + (GQA_KV_HEADS, HEAD_DIM)), gqa_k_norm)
    va = va.reshape(lead + (GQA_KV_HEADS, HEAD_DIM))
    kv = (rms_norm(ckv, mla_kv_lora_norm) @ w_ukv).reshape(lead + (MLA_HEADS, MLA_NOPE + MLA_V))
    k_nope, vc = kv[..., :MLA_NOPE], kv[..., MLA_NOPE:]
    krope_h = jnp.broadcast_to(krope[..., None, :], lead + (MLA_HEADS, MLA_ROPE))
    kc = rms_norm(jnp.concatenate([k_nope, krope_h], axis=-1), mla_k_norm)
    return ka, va, kc, vc


def attn_queries(p_q, gqa_q_norm, mla_q_lora_norm, w_uq, mla_q_norm):
    qa, cq = split_cols(p_q, COL_SIZES[4:6])
    lead = p_q.shape[:-1]
    qa = rms_norm(qa.reshape(lead + (GQA_KV_HEADS, GQA_GROUP, HEAD_DIM)), gqa_q_norm)
    qc = (rms_norm(cq, mla_q_lora_norm) @ w_uq).reshape(lead + (MLA_HEADS, 1, MLA_QK))
    qc = rms_norm(qc, mla_q_norm)
    return qa, qc


def dwconv_centred(z, w):
    n = z.shape[1]
    pad = CONV_WIDTH // 2
    zp = jnp.pad(z, ((0, 0), (pad, pad), (0, 0)))
    return sum(zp[:, j:j + n] * w[j] for j in range(CONV_WIDTH))


def merge_branches(p_rest, y_a, y_c, conv_w, w_out_a, w_out_b, w_out_c, b_gate, w_o):
    xb, gb, gc, g_lin = split_cols(p_rest, COL_SIZES[6:])
    y_b = gb * dwconv_centred(gc * xb, conv_w)
    g_a, g_b, g_c = jnp.split(jax.nn.sigmoid(g_lin + b_gate), N_BRANCH, axis=-1)
    m = g_a * (y_a @ w_out_a) + g_b * (y_b @ w_out_b) + g_c * (y_c @ w_out_c)
    return m @ w_o


def moe(h, w_group, b_group, w_router, b_router, w_gu, w_down):
    shape = h.shape
    t = h.reshape(-1, shape[-1])
    n_tok = t.shape[0]
    n_assign = n_tok * TOP_K
    p_grp = jax.nn.softmax((t @ w_group).astype(jnp.float32) + b_group.astype(jnp.float32), axis=-1)
    g_prob, g_idx = lax.top_k(p_grp, 1)
    logits = ((t @ w_router).astype(jnp.float32) + b_router.astype(jnp.float32))
    logits = logits.reshape(n_tok, N_GROUPS, EXPERTS_PER_GROUP)
    sel = jnp.broadcast_to(g_idx[:, :, None], (n_tok, 1, EXPERTS_PER_GROUP))
    in_grp = jnp.take_along_axis(logits, sel, axis=1)[:, 0]
    e_logit, e_local = lax.top_k(in_grp, TOP_K)
    comb = jax.nn.softmax(e_logit, axis=-1) * g_prob
    expert = (g_idx * EXPERTS_PER_GROUP + e_local).reshape(-1)
    order = jnp.argsort(expert)
    e_sorted = expert[order]
    tok_sorted = (order // TOP_K).astype(jnp.int32)
    w_sorted = comb.reshape(-1)[order].astype(t.dtype)
    sizes = jnp.bincount(expert, length=N_EXPERTS)
    padded = (sizes + MOE_BLOCK - 1) // MOE_BLOCK * MOE_BLOCK
    pad_end = jnp.cumsum(padded)
    pad_start = pad_end - padded
    start = jnp.cumsum(sizes) - sizes
    dest = pad_start[e_sorted] + jnp.arange(n_assign) - start[e_sorted]
    n_slots = (n_assign + MOE_BLOCK - 1) // MOE_BLOCK * MOE_BLOCK + N_EXPERTS * MOE_BLOCK
    n_blocks = n_slots // MOE_BLOCK
    slot_tok = jnp.zeros((n_slots,), jnp.int32).at[dest].set(tok_sorted)
    slot_w = jnp.zeros((n_slots,), t.dtype).at[dest].set(w_sorted)
    block_expert = jnp.minimum(
        jnp.searchsorted(pad_end, jnp.arange(n_blocks) * MOE_BLOCK, side='right'), N_EXPERTS - 1)
    xs = t[slot_tok].reshape(n_blocks, MOE_BLOCK, shape[-1])

    def expert_block(args):
        xblk, e = args
        gu = xblk @ w_gu[e]
        return (jax.nn.silu(gu[:, :D_EXPERT]) * gu[:, D_EXPERT:]) @ w_down[e]

    ys = lax.map(expert_block, (xs, block_expert)).reshape(n_slots, shape[-1]) * slot_w[:, None]
    return jnp.zeros_like(t).at[slot_tok].add(ys).reshape(shape)


def setup_inputs(seed: int = 0) -> dict:
    key = jax.random.key(seed)
    ks = iter(jax.random.split(key, 40))

    def nrm(shape, scale):
        return jax.random.normal(next(ks), shape, jnp.float32) * scale

    def gain(shape):
        return 1.0 + 0.02 * jax.random.normal(next(ks), shape, jnp.float32)

    L, D = DEPTH, D_MODEL
    return {
        'x': nrm((BATCH, SEQ, D), 1.0),
        'c': nrm((BATCH, D), 1.0),
        'ctx': nrm((BATCH, CTX_LEN, D), 1.0),
        'c_ctx': nrm((D,), 1.0),
        'w_mod': nrm((L, D, 6 * D), D ** -0.5),
        'b_mod': nrm((L, 6 * D), 0.01),
        'norm1': gain((L, D)),
        'norm2': gain((L, D)),
        'w_in': nrm((L, D, IN_COLS), D ** -0.5),
        'gqa_q_norm': gain((L, HEAD_DIM)),
        'gqa_k_norm': gain((L, HEAD_DIM)),
        'mla_q_lora_norm': gain((L, MLA_Q_LORA)),
        'w_uq': nrm((L, MLA_Q_LORA, MLA_HEADS * MLA_QK), MLA_Q_LORA ** -0.5),
        'mla_kv_lora_norm': gain((L, MLA_KV_LORA)),
        'w_ukv': nrm((L, MLA_KV_LORA, MLA_HEADS * (MLA_NOPE + MLA_V)), MLA_KV_LORA ** -0.5),
        'mla_q_norm': gain((L, MLA_QK)),
        'mla_k_norm': gain((L, MLA_QK)),
        'conv_w': nrm((L, CONV_WIDTH, CONV_DIM), CONV_WIDTH ** -0.5),
        'w_out_a': nrm((L, GQA_HEADS * HEAD_DIM, D), (GQA_HEADS * HEAD_DIM) ** -0.5),
        'w_out_b': nrm((L, CONV_DIM, D), CONV_DIM ** -0.5),
        'w_out_c': nrm((L, MLA_HEADS * MLA_V, D), (MLA_HEADS * MLA_V) ** -0.5),
        'b_gate': nrm((L, N_BRANCH * D), 0.01),
        'w_o': nrm((L, D, D), D ** -0.5),
        'w_group': nrm((L, D, N_GROUPS), D ** -0.5),
        'b_group': nrm((L, N_GROUPS), 0.01),
        'w_router': nrm((L, D, N_EXPERTS), D ** -0.5),
        'b_router': nrm((L, N_EXPERTS), 0.01),
        'w_gu': nrm((L, N_EXPERTS, D, 2 * D_EXPERT), D ** -0.5),
        'w_down': nrm((L, N_EXPERTS, D_EXPERT, D), D_EXPERT ** -0.5),
    }


def reference(x, c, ctx, c_ctx, w_mod, b_mod, norm1, norm2, w_in, gqa_q_norm, gqa_k_norm,
              mla_q_lora_norm, w_uq, mla_kv_lora_norm, w_ukv, mla_q_norm, mla_k_norm,
              conv_w, w_out_a, w_out_b, w_out_c, b_gate, w_o,
              w_group, b_group, w_router, b_router, w_gu, w_down):
    n_lat = x.shape[1]
    rows_n = n_lat // GRID_W
    rows = jnp.repeat(jnp.arange(rows_n, dtype=jnp.float32), GRID_W)
    cols = jnp.tile(jnp.arange(GRID_W, dtype=jnp.float32), rows_n)
    cos_a, sin_a = rope_tables(rows, cols, HEAD_DIM)
    cos_c, sin_c = rope_tables(rows, cols, MLA_ROPE)
    c_act = jax.nn.silu(c)[:, None, :]
    cc_act = jax.nn.silu(c_ctx)[None, None, :]
    for i in range(DEPTH):
        last = i == DEPTH - 1
        sh1, sc1, g1, sh2, sc2, g2 = jnp.split(c_act @ w_mod[i] + b_mod[i], 6, axis=-1)
        xsh1, xsc1, xg1, xsh2, xsc2, xg2 = jnp.split(cc_act @ w_mod[i] + b_mod[i], 6, axis=-1)
        kv_args = (gqa_k_norm[i], mla_kv_lora_norm[i], w_ukv[i], mla_k_norm[i])
        q_args = (gqa_q_norm[i], mla_q_lora_norm[i], w_uq[i], mla_q_norm[i])
        mix_args = (conv_w[i], w_out_a[i], w_out_b[i], w_out_c[i], b_gate[i], w_o[i])
        moe_args = (w_group[i], b_group[i], w_router[i], b_router[i], w_gu[i], w_down[i])
        h = rms_norm(x, norm1[i]) * (1 + sc1) + sh1
        hc = rms_norm(ctx, norm1[i]) * (1 + xsc1) + xsh1
        p = h @ w_in[i]
        pc = hc @ (w_in[i][:, :KV_COLS] if last else w_in[i])
        ka_x, va_x, kc_x, vc_x = attn_keys(pc[..., :KV_COLS], *kv_args)
        ka, va, kc, vc = attn_keys(p[..., :KV_COLS], *kv_args)
        qa, qc = attn_queries(p[..., KV_COLS:REST_OFF], *q_args)
        ka = apply_rope(ka, cos_a, sin_a)
        qa = apply_rope(qa, cos_a, sin_a)
        kc = rope_tail(kc, cos_c, sin_c)
        qc = rope_tail(qc, cos_c, sin_c)
        y_a = blocked_attention(qa, jnp.concatenate([ka_x, ka], axis=1), jnp.concatenate([va_x, va], axis=1))
        y_c = blocked_attention(qc, jnp.concatenate([kc_x, kc], axis=1), jnp.concatenate([vc_x, vc], axis=1))
        x = x + g1 * merge_branches(p[..., REST_OFF:], y_a, y_c, *mix_args)
        if not last:
            qa_x, qc_x = attn_queries(pc[..., KV_COLS:REST_OFF], *q_args)
            y_ax = blocked_attention(qa_x, ka_x, va_x)
            y_cx = blocked_attention(qc_x, kc_x, vc_x)
            ctx = ctx + xg1 * merge_branches(pc[..., REST_OFF:], y_ax, y_cx, *mix_args)
            hc2 = rms_norm(ctx, norm2[i]) * (1 + xsc2) + xsh2
            ctx = ctx + xg2 * moe(hc2, *moe_args)
        h2 = rms_norm(x, norm2[i]) * (1 + sc2) + sh2
        x = x + g2 * moe(h2, *moe_args)
    return x
```

```python
import functools

import numpy as np
import jax
import jax.numpy as jnp
from jax import lax
from jax.experimental import pallas as pl
from jax.experimental.pallas import tpu as pltpu

D = 2048
BATCH = 4
SEQ = 4096
DEPTH = 2
GRID_W = 64
CTX = 256
ROPE_THETA = 10000.0
EPS = 1e-6
KVH = 2
GROUP = 4
HD = 128
CONV_DIM = 1024
MLA_H = 8
Q_LORA = 512
KV_LORA = 256
NOPE = 128
ROPE = 64
MLA_V = 128
MLA_QK = NOPE + ROPE
MLA_PAD = 256
N_GROUPS = 4
EPG = 8
N_EXPERTS = N_GROUPS * EPG
TOP_K = 2
D_EXPERT = 512
MOE_BLOCK = 256

T_LAT = BATCH * SEQ
T_CTX = BATCH * CTX
T_ALL = T_LAT + T_CTX
KV_COLS = 2 * KVH * HD + KV_LORA + ROPE
Q_COLS = KVH * GROUP * HD + Q_LORA
REST_OFF = KV_COLS + Q_COLS
KVQ_PAD = 1024 + Q_COLS
REST_COLS = 3 * CONV_DIM + 3 * D
ROUTE_PAD = 128

V7X_VMEM_LIMIT = 56 * 1024 * 1024
TM = 1024
TQ = 256
BF16_SUBLANES = 16

F32 = jnp.float32
BF16 = jnp.bfloat16


def _params(*sem):
    return pltpu.CompilerParams(dimension_semantics=sem, vmem_limit_bytes=V7X_VMEM_LIMIT)


def _dot(a, b):
    return jnp.dot(a, b, preferred_element_type=F32)


def _mod_kernel(c_ref, w_ref, b_ref, o_ref):
    c = c_ref[...]
    a = (c * jax.nn.sigmoid(c)).astype(BF16)
    o_ref[...] = _dot(a, w_ref[...].astype(BF16)) + b_ref[...]


def _modulation(cin, w_mod, b_mod):
    tn = 1024
    return pl.pallas_call(
        _mod_kernel,
        out_shape=jax.ShapeDtypeStruct((DEPTH, 8, 6 * D), F32),
        grid=(DEPTH, 6 * D // tn),
        in_specs=[
            pl.BlockSpec((8, D), lambda l, j: (0, 0)),
            pl.BlockSpec((None, D, tn), lambda l, j: (l, 0, j)),
            pl.BlockSpec((None, 1, tn), lambda l, j: (l, 0, j)),
        ],
        out_specs=pl.BlockSpec((None, 8, tn), lambda l, j: (l, 0, j)),
        compiler_params=_params("parallel", "parallel"),
        name="modulation",
    )(cin, w_mod, b_mod.reshape(DEPTH, 1, 6 * D))


def _norm_mod_rows(x, g, sc, sh):
    ms = jnp.mean(x * x, axis=-1, keepdims=True)
    return (x * lax.rsqrt(ms + EPS) * g) * (1.0 + sc) + sh


def _normed_matmul_kernel(x_ref, g_ref, sc_ref, sh_ref, w_ref, o_ref, h_scr):
    rows = 128

    @pl.when(pl.program_id(1) == 0)
    def _():
        def body(r, carry):
            sl = pl.ds(pl.multiple_of(r * rows, rows), rows)
            h_scr[sl, :] = _norm_mod_rows(x_ref[sl, :], g_ref[...], sc_ref[...], sh_ref[...]).astype(BF16)
            return carry

        lax.fori_loop(0, TM // rows, body, 0)

    o_ref[...] = _dot(h_scr[...], w_ref[...]).astype(o_ref.dtype)


def _normed_matmul(xs, gain, mod, sc_chunk, sh_chunk, w, n_rows, tn, name):
    n = w.shape[1]
    return pl.pallas_call(
        _normed_matmul_kernel,
        out_shape=jax.ShapeDtypeStruct((n_rows, n), BF16),
        grid=(n_rows // TM, n // tn),
        in_specs=[
            pl.BlockSpec((TM, D), lambda i, j: (i, 0)),
            pl.BlockSpec((1, D), lambda i, j: (0, 0)),
            pl.BlockSpec((None, 1, D), lambda i, j: (i // (SEQ // TM), 0, sc_chunk)),
            pl.BlockSpec((None, 1, D), lambda i, j: (i // (SEQ // TM), 0, sh_chunk)),
            pl.BlockSpec((D, tn), lambda i, j: (0, j)),
        ],
        out_specs=pl.BlockSpec((TM, tn), lambda i, j: (i, j)),
        scratch_shapes=[pltpu.VMEM((TM, D), BF16)],
        compiler_params=_params("parallel", "arbitrary"),
        name=name,
    )(xs, gain.reshape(1, D), mod, mod, w)


def _rms_lanes(x, width):
    return lax.rsqrt(jnp.sum(x * x, axis=-1, keepdims=True) * (1.0 / width) + EPS)


def _prep_kernel(p_ref, ra_ref, rc_ref, gq_ref, gk_ref, gql_ref, gkl_ref, gmq_ref, gmk_ref, wuq_ref, wukv_ref,
                 ka_ref, va_ref, kc_ref, vc_ref, qa_ref, qc_ref):
    cos_a = ra_ref[:, 0:HD]
    sin_a = ra_ref[:, HD:2 * HD]
    cos_c = rc_ref[:, 0:128]
    sin_up = rc_ref[:, 128:256]
    sin_dn = rc_ref[:, 256:384]

    def rope_a(x):
        return x * cos_a + pltpu.roll(x, HD // 2, 1) * sin_a

    def rope_c(x):
        return x * cos_c + pltpu.roll(x, 128 - ROPE // 2, 1) * sin_up + pltpu.roll(x, ROPE // 2, 1) * sin_dn

    for h in range(KVH):
        k = p_ref[:, h * HD:(h + 1) * HD].astype(F32)
        ka_ref[:, h * HD:(h + 1) * HD] = rope_a(k * _rms_lanes(k, HD) * gk_ref[...]).astype(BF16)
    va_ref[...] = p_ref[:, KVH * HD:2 * KVH * HD]

    ckv = p_ref[:, 512:768].astype(F32)
    ckv_n = (ckv * _rms_lanes(ckv, KV_LORA) * gkl_ref[...]).astype(BF16)
    kv = _dot(ckv_n, wukv_ref[...])
    krope = p_ref[:, 768:896].astype(F32)
    krope_ss = jnp.sum(krope * krope, axis=-1, keepdims=True)
    krope_rot = rope_c(krope * gmk_ref[:, 128:256])
    for h in range(MLA_H):
        kn = kv[:, h * 256:h * 256 + NOPE]
        r = lax.rsqrt((jnp.sum(kn * kn, axis=-1, keepdims=True) + krope_ss) * (1.0 / MLA_QK) + EPS)
        kc_ref[:, h * MLA_PAD:h * MLA_PAD + NOPE] = (kn * r * gmk_ref[:, 0:128]).astype(BF16)
        kc_ref[:, h * MLA_PAD + NOPE:(h + 1) * MLA_PAD] = (krope_rot * r).astype(BF16)
        vc_ref[:, h * MLA_V:(h + 1) * MLA_V] = kv[:, h * 256 + NOPE:(h + 1) * 256].astype(BF16)

    for h in range(KVH * GROUP):
        q = p_ref[:, 1024 + h * HD:1024 + (h + 1) * HD].astype(F32)
        q = rope_a(q * _rms_lanes(q, HD) * gq_ref[...]) * (HD ** -0.5)
        qa_ref[:, h * HD:(h + 1) * HD] = q.astype(BF16)

    cq = p_ref[:, 2048:2048 + Q_LORA].astype(F32)
    cq_n = (cq * _rms_lanes(cq, Q_LORA) * gql_ref[...]).astype(BF16)
    qc = _dot(cq_n, wuq_ref[...])
    for h in range(MLA_H):
        qn = qc[:, h * MLA_PAD:h * MLA_PAD + NOPE]
        qr = qc[:, h * MLA_PAD + NOPE:(h + 1) * MLA_PAD]
        r = lax.rsqrt((jnp.sum(qn * qn, axis=-1, keepdims=True) + jnp.sum(qr * qr, axis=-1, keepdims=True))
                      * (1.0 / MLA_QK) + EPS) * (MLA_QK ** -0.5)
        qc_ref[:, h * MLA_PAD:h * MLA_PAD + NOPE] = (qn * r * gmq_ref[:, 0:128]).astype(BF16)
        qc_ref[:, h * MLA_PAD + NOPE:(h + 1) * MLA_PAD] = (rope_c(qr * gmq_ref[:, 128:256]) * r).astype(BF16)


def _attention_prep(pkvq, rope_a, rope_c, gq, gk, gql, gkl, gmq, gmk, wuq, wukv):
    tm = 256
    n_lat = T_LAT // tm
    per_seq = SEQ // tm

    def rope_idx(i):
        return jnp.where(i < n_lat, i % per_seq, per_seq + i - n_lat)

    def full(shape):
        return pl.BlockSpec(shape, lambda i: (0, 0))

    def rows(width):
        return pl.BlockSpec((tm, width), lambda i: (i, 0))

    out_widths = (KVH * HD, KVH * HD, MLA_H * MLA_PAD, MLA_H * MLA_V, KVH * GROUP * HD, MLA_H * MLA_PAD)
    return pl.pallas_call(
        _prep_kernel,
        out_shape=[jax.ShapeDtypeStruct((T_ALL, w), BF16) for w in out_widths],
        grid=(T_ALL // tm,),
        in_specs=[
            rows(KVQ_PAD),
            pl.BlockSpec((tm, 2 * HD), lambda i: (rope_idx(i), 0)),
            pl.BlockSpec((tm, 3 * 128), lambda i: (rope_idx(i), 0)),
            full((1, HD)), full((1, HD)), full((1, Q_LORA)), full((1, KV_LORA)),
            full((1, MLA_PAD)), full((1, MLA_PAD)),
            full((Q_LORA, MLA_H * MLA_PAD)), full((KV_LORA, MLA_H * (NOPE + MLA_V))),
        ],
        out_specs=[rows(w) for w in out_widths],
        compiler_params=_params("parallel"),
        name="attention_prep",
    )(pkvq, rope_a, rope_c, gq, gk, gql, gkl, gmq, gmk, wuq, wukv)


def _attn_kernel(q_ref, kl_ref, vl_ref, kx_ref, vx_ref, o_ref, *, group, dk, dv, chunk, n_lat_tiles):
    tq = q_ref.shape[0]
    if group > 1:
        q = jnp.concatenate([q_ref[:, g * dk:(g + 1) * dk] for g in range(group)], axis=0)
    else:
        q = q_ref[...]
    rows = q.shape[0]

    def run(segments):
        m = jnp.full((rows, 1), -jnp.inf, F32)
        l = jnp.zeros((rows, 1), F32)
        acc = jnp.zeros((rows, dv), F32)
        for k_ref, v_ref in segments:
            n_keys = k_ref.shape[0]
            step = min(chunk, n_keys)
            for c0 in range(0, n_keys, step):
                s = lax.dot_general(q, k_ref[c0:c0 + step, :], (((1,), (1,)), ((), ())),
                                    preferred_element_type=F32)
                m_new = jnp.maximum(m, jnp.max(s, axis=-1, keepdims=True))
                alpha = jnp.exp(m - m_new)
                p = jnp.exp(s - m_new)
                l = alpha * l + jnp.sum(p, axis=-1, keepdims=True)
                acc = alpha * acc + _dot(p.astype(BF16), v_ref[c0:c0 + step, :])
                m = m_new
        o = acc / l
        for g in range(group):
            o_ref[:, g * dv:(g + 1) * dv] = o[g * tq:(g + 1) * tq].astype(o_ref.dtype)

    qi = pl.program_id(2)

    @pl.when(qi < n_lat_tiles)
    def _():
        run(((kx_ref, vx_ref), (kl_ref, vl_ref)))

    @pl.when(qi >= n_lat_tiles)
    def _():
        run(((kx_ref, vx_ref),))


def _attention(q, k, v, *, heads, group, dk, dv, chunk, with_ctx_queries, name):
    n_lat_tiles = SEQ // TQ
    ctx_blk0 = T_LAT // TQ
    n_q = n_lat_tiles + (1 if with_ctx_queries else 0)

    def q_idx(b, h, qi):
        return (jnp.where(qi < n_lat_tiles, b * n_lat_tiles + qi, ctx_blk0 + b), h)

    kern = functools.partial(_attn_kernel, group=group, dk=dk, dv=dv, chunk=chunk, n_lat_tiles=n_lat_tiles)
    return pl.pallas_call(
        kern,
        out_shape=jax.ShapeDtypeStruct((T_ALL if with_ctx_queries else T_LAT, heads * group * dv), BF16),
        grid=(BATCH, heads, n_q),
        in_specs=[
            pl.BlockSpec((TQ, group * dk), q_idx),
            pl.BlockSpec((SEQ, dk), lambda b, h, qi: (b, h)),
            pl.BlockSpec((SEQ, dv), lambda b, h, qi: (b, h)),
            pl.BlockSpec((CTX, dk), lambda b, h, qi: (T_LAT // CTX + b, h)),
            pl.BlockSpec((CTX, dv), lambda b, h, qi: (T_LAT // CTX + b, h)),
        ],
        out_specs=pl.BlockSpec((TQ, group * dv), q_idx),
        compiler_params=_params("parallel", "parallel", "arbitrary"),
        name=name,
    )(q, k, v, k, v)


def _conv_kernel(xb_ref, gb_ref, gc_ref, xbp_ref, gcp_ref, xbn_ref, gcn_ref, w_ref, o_ref):
    i = pl.program_id(0)
    z = gc_ref[...].astype(F32) * xb_ref[...].astype(F32)
    last = BF16_SUBLANES - 1
    z_prev = gcp_ref[last:last + 1, :].astype(F32) * xbp_ref[last:last + 1, :].astype(F32)
    z_next = gcn_ref[0:1, :].astype(F32) * xbn_ref[0:1, :].astype(F32)
    row = lax.broadcasted_iota(jnp.int32, (TM, 1), 0)
    seq_len = jnp.where(i < T_LAT // TM, SEQ, CTX)
    pos = (i * TM + row) & (seq_len - 1)
    z_dn = jnp.where(row == 0, z_prev, pltpu.roll(z, 1, 0))
    z_up = jnp.where(row == TM - 1, z_next, pltpu.roll(z, TM - 1, 0))
    z_dn = jnp.where(pos == 0, 0.0, z_dn)
    z_up = jnp.where(pos == seq_len - 1, 0.0, z_up)
    y = gb_ref[...].astype(F32) * (w_ref[0:1, :] * z_dn + w_ref[1:2, :] * z + w_ref[2:3, :] * z_up)
    o_ref[...] = y.astype(BF16)


def _conv_branch(prest, conv_w, n_rows):
    tc = 256
    n_ct = CONV_DIM // tc
    halo = TM // BF16_SUBLANES
    last_halo = n_rows // BF16_SUBLANES - 1

    def main(col0):
        return pl.BlockSpec((TM, tc), lambda i, j: (i, col0 * n_ct + j))

    def prev(col0):
        return pl.BlockSpec((BF16_SUBLANES, tc), lambda i, j: (jnp.maximum(i * halo - 1, 0), col0 * n_ct + j))

    def nxt(col0):
        return pl.BlockSpec((BF16_SUBLANES, tc), lambda i, j: (jnp.minimum((i + 1) * halo, last_halo), col0 * n_ct + j))

    return pl.pallas_call(
        _conv_kernel,
        out_shape=jax.ShapeDtypeStruct((n_rows, CONV_DIM), BF16),
        grid=(n_rows // TM, n_ct),
        in_specs=[main(0), main(1), main(2), prev(0), prev(2), nxt(0), nxt(2),
                  pl.BlockSpec((3, tc), lambda i, j: (0, j))],
        out_specs=pl.BlockSpec((TM, tc), lambda i, j: (i, j)),
        compiler_params=_params("parallel", "parallel"),
        name="conv_branch",
    )(prest, prest, prest, prest, prest, prest, prest, conv_w)


def _merge_kernel(ya_ref, yb_ref, yc_ref, ga_ref, gb_ref, gc_ref, ba_ref, bb_ref, bc_ref,
                  wa_ref, wb_ref, wc_ref, o_ref):
    def branch(y_ref, w_ref, g_ref, b_ref):
        return jax.nn.sigmoid(g_ref[...].astype(F32) + b_ref[...]) * _dot(y_ref[...], w_ref[...])

    m = branch(ya_ref, wa_ref, ga_ref, ba_ref)
    m = m + branch(yb_ref, wb_ref, gb_ref, bb_ref)
    m = m + branch(yc_ref, wc_ref, gc_ref, bc_ref)
    o_ref[...] = m.astype(BF16)


def _merge(ya, yb, yc, prest, b_gate, wa, wb, wc, n_rows):
    tn = 512
    n_ct = D // tn
    gate_blk0 = 3 * CONV_DIM // tn

    def y_spec():
        return pl.BlockSpec((TM, CONV_DIM), lambda i, j: (i, 0))

    def gate_spec(k):
        return pl.BlockSpec((TM, tn), lambda i, j: (i, gate_blk0 + k * n_ct + j))

    def bias_spec(k):
        return pl.BlockSpec((1, tn), lambda i, j: (0, k * n_ct + j))

    def w_spec():
        return pl.BlockSpec((CONV_DIM, tn), lambda i, j: (0, j))

    return pl.pallas_call(
        _merge_kernel,
        out_shape=jax.ShapeDtypeStruct((n_rows, D), BF16),
        grid=(n_rows // TM, n_ct),
        in_specs=[y_spec(), y_spec(), y_spec(), gate_spec(0), gate_spec(1), gate_spec(2),
                  bias_spec(0), bias_spec(1), bias_spec(2), w_spec(), w_spec(), w_spec()],
        out_specs=pl.BlockSpec((TM, tn), lambda i, j: (i, j)),
        compiler_params=_params("parallel", "arbitrary"),
        name="merge_branches",
    )(ya, yb, yc, prest, prest, prest, b_gate, b_gate, b_gate, wa, wb, wc)


def _oproj_kernel(m_ref, w_ref, x_ref, g_ref, o_ref):
    o_ref[...] = x_ref[...] + g_ref[...] * _dot(m_ref[...], w_ref[...])


def _oproj_residual(m, w_o, xs, mod, gate_chunk, n_rows):
    tn = 512
    n_ct = D // tn
    return pl.pallas_call(
        _oproj_kernel,
        out_shape=jax.ShapeDtypeStruct((n_rows, D), F32),
        grid=(n_rows // TM, n_ct),
        in_specs=[
            pl.BlockSpec((TM, D), lambda i, j: (i, 0)),
            pl.BlockSpec((D, tn), lambda i, j: (0, j)),
            pl.BlockSpec((TM, tn), lambda i, j: (i, j)),
            pl.BlockSpec((None, 1, tn), lambda i, j: (i // (SEQ // TM), 0, gate_chunk * n_ct + j)),
        ],
        out_specs=pl.BlockSpec((TM, tn), lambda i, j: (i, j)),
        compiler_params=_params("parallel", "arbitrary"),
        name="oproj_residual",
    )(m, w_o, xs, mod)


def _router_kernel(x_ref, g_ref, sc_ref, sh_ref, whi_ref, wlo_ref, b_ref, h_ref, lg_ref):
    rows = 128

    def body(r, carry):
        sl = pl.ds(pl.multiple_of(r * rows, rows), rows)
        h = _norm_mod_rows(x_ref[sl, :], g_ref[...], sc_ref[...], sh_ref[...])
        h_ref[sl, :] = h
        h_hi = h.astype(BF16)
        h_lo = (h - h_hi.astype(F32)).astype(BF16)
        lg = _dot(h_hi, whi_ref[...]) + (_dot(h_hi, wlo_ref[...]) + _dot(h_lo, whi_ref[...]))
        lg_ref[sl, :] = lg + b_ref[...]
        return carry

    lax.fori_loop(0, TM // rows, body, 0)


def _router(xs, gain, mod, sc_chunk, sh_chunk, w_hi, w_lo, bias, n_rows):
    return pl.pallas_call(
        _router_kernel,
        out_shape=[jax.ShapeDtypeStruct((n_rows, D), F32), jax.ShapeDtypeStruct((n_rows, ROUTE_PAD), F32)],
        grid=(n_rows // TM,),
        in_specs=[
            pl.BlockSpec((TM, D), lambda i: (i, 0)),
            pl.BlockSpec((1, D), lambda i: (0, 0)),
            pl.BlockSpec((None, 1, D), lambda i: (i // (SEQ // TM), 0, sc_chunk)),
            pl.BlockSpec((None, 1, D), lambda i: (i // (SEQ // TM), 0, sh_chunk)),
            pl.BlockSpec((D, ROUTE_PAD), lambda i: (0, 0)),
            pl.BlockSpec((D, ROUTE_PAD), lambda i: (0, 0)),
            pl.BlockSpec((1, ROUTE_PAD), lambda i: (0, 0)),
        ],
        out_specs=[pl.BlockSpec((TM, D), lambda i: (i, 0)), pl.BlockSpec((TM, ROUTE_PAD), lambda i: (i, 0))],
        compiler_params=_params("parallel"),
        name="moe_router",
    )(xs, gain.reshape(1, D), mod, mod, w_hi, w_lo, bias)


def _gather_rows(src_hbm, dst, sem, n, index_of):
    def issue(r, carry):
        pltpu.make_async_copy(src_hbm.at[pl.ds(index_of(r), 1), :], dst.at[pl.ds(r, 1), :], sem).start()
        return carry

    lax.fori_loop(0, n, issue, 0, unroll=8)

    def drain(r, carry):
        pltpu.make_async_copy(src_hbm.at[pl.ds(0, 1), :], dst.at[pl.ds(r, 1), :], sem).wait()
        return carry

    lax.fori_loop(0, n, drain, 0, unroll=8)


def _expert_kernel(be_ref, tok_ref, nused_ref, h_hbm, wgu_ref, wdn_ref, sw_ref, o_ref, xbuf, sem):
    i = pl.program_id(0)

    @pl.when(i < nused_ref[0])
    def _():
        _gather_rows(h_hbm, xbuf, sem, MOE_BLOCK, lambda r: tok_ref[i * MOE_BLOCK + r])
        gu = _dot(xbuf[...].astype(BF16), wgu_ref[...])
        gate = gu[:, :D_EXPERT]
        act = gate * jax.nn.sigmoid(gate) * gu[:, D_EXPERT:]
        o_ref[...] = _dot(act.astype(BF16), wdn_ref[...]) * sw_ref[...]

    @pl.when(i >= nused_ref[0])
    def _():
        o_ref[...] = jnp.zeros_like(o_ref)


def _experts(block_expert, slot_tok, n_used, h2, w_gu, w_down, slot_w, n_blocks):
    grid_spec = pltpu.PrefetchScalarGridSpec(
        num_scalar_prefetch=3,
        grid=(n_blocks,),
        in_specs=[
            pl.BlockSpec(memory_space=pl.ANY),
            pl.BlockSpec((None, D, 2 * D_EXPERT), lambda i, be, tok, nu: (be[i], 0, 0)),
            pl.BlockSpec((None, D_EXPERT, D), lambda i, be, tok, nu: (be[i], 0, 0)),
            pl.BlockSpec((MOE_BLOCK, 1), lambda i, be, tok, nu: (i, 0)),
        ],
        out_specs=pl.BlockSpec((MOE_BLOCK, D), lambda i, be, tok, nu: (i, 0)),
        scratch_shapes=[pltpu.VMEM((MOE_BLOCK, D), F32), pltpu.SemaphoreType.DMA(())],
    )
    return pl.pallas_call(
        _expert_kernel,
        out_shape=jax.ShapeDtypeStruct((n_blocks * MOE_BLOCK, D), F32),
        grid_spec=grid_spec,
        compiler_params=_params("arbitrary"),
        name="moe_experts",
    )(block_expert, slot_tok, n_used, h2, w_gu, w_down, slot_w)


def _combine_kernel(pos_ref, ys_hbm, x_ref, g_ref, o_ref, buf, sem):
    i = pl.program_id(0)
    rows = x_ref.shape[0]
    _gather_rows(ys_hbm, buf, sem, TOP_K * rows,
                 lambda r: pos_ref[(i * rows + (r & (rows - 1))) * TOP_K + (r >> int(np.log2(rows)))])
    o_ref[...] = x_ref[...] + g_ref[...] * (buf[0:rows, :] + buf[rows:2 * rows, :])


def _combine(pos, ys, xs, mod, gate_chunk, n_rows):
    rows = 256
    grid_spec = pltpu.PrefetchScalarGridSpec(
        num_scalar_prefetch=1,
        grid=(n_rows // rows,),
        in_specs=[
            pl.BlockSpec(memory_space=pl.ANY),
            pl.BlockSpec((rows, D), lambda i, pos: (i, 0)),
            pl.BlockSpec((None, 1, D), lambda i, pos: (i // (SEQ // rows), 0, gate_chunk)),
        ],
        out_specs=pl.BlockSpec((rows, D), lambda i, pos: (i, 0)),
        scratch_shapes=[pltpu.VMEM((TOP_K * rows, D), F32), pltpu.SemaphoreType.DMA(())],
    )
    return pl.pallas_call(
        _combine_kernel,
        out_shape=jax.ShapeDtypeStruct((n_rows, D), F32),
        grid_spec=grid_spec,
        compiler_params=_params("arbitrary"),
        name="moe_combine",
    )(pos, ys, xs, mod)


def _route(logits, n_tok):
    n_assign = n_tok * TOP_K
    p_grp = jax.nn.softmax(logits[:, :N_GROUPS], axis=-1)
    g_prob, g_idx = lax.top_k(p_grp, 1)
    fine = logits[:, N_GROUPS:N_GROUPS + N_EXPERTS].reshape(n_tok, N_GROUPS, EPG)
    sel = jnp.broadcast_to(g_idx[:, :, None], (n_tok, 1, EPG))
    in_grp = jnp.take_along_axis(fine, sel, axis=1)[:, 0]
    e_logit, e_local = lax.top_k(in_grp, TOP_K)
    comb = jax.nn.softmax(e_logit, axis=-1) * g_prob
    expert = (g_idx * EPG + e_local).reshape(-1).astype(jnp.int32)
    order = jnp.argsort(expert)
    e_sorted = expert[order]
    tok_sorted = (order // TOP_K).astype(jnp.int32)
    w_sorted = comb.reshape(-1)[order]
    sizes = jnp.bincount(expert, length=N_EXPERTS).astype(jnp.int32)
    padded = (sizes + MOE_BLOCK - 1) // MOE_BLOCK * MOE_BLOCK
    pad_end = jnp.cumsum(padded)
    pad_start = pad_end - padded
    start = jnp.cumsum(sizes) - sizes
    dest = (pad_start[e_sorted] + jnp.arange(n_assign, dtype=jnp.int32) - start[e_sorted]).astype(jnp.int32)
    n_slots = (n_assign + MOE_BLOCK - 1) // MOE_BLOCK * MOE_BLOCK + N_EXPERTS * MOE_BLOCK
    n_blocks = n_slots // MOE_BLOCK
    slot_tok = jnp.zeros((n_slots,), jnp.int32).at[dest].set(tok_sorted)
    slot_w = jnp.zeros((n_slots,), F32).at[dest].set(w_sorted)
    block_expert = jnp.minimum(
        jnp.searchsorted(pad_end, jnp.arange(n_blocks, dtype=jnp.int32) * MOE_BLOCK, side='right'),
        N_EXPERTS - 1).astype(jnp.int32)
    pos = jnp.zeros((n_assign,), jnp.int32).at[order].set(dest)
    n_used = (pad_end[-1] // MOE_BLOCK).astype(jnp.int32).reshape(1)
    return block_expert, slot_tok, slot_w.reshape(n_slots, 1), pos, n_used, n_blocks


def _rope_tables():
    s = jnp.arange(SEQ)
    rows = (s // GRID_W).astype(F32)
    cols = (s % GRID_W).astype(F32)

    def angles(rot_dim):
        n = rot_dim // 4
        inv = ROPE_THETA ** (-jnp.arange(n, dtype=F32) / n)
        return jnp.concatenate([rows[:, None] * inv, cols[:, None] * inv], axis=-1)

    ang_a = angles(HD)
    cos_a, sin_a = jnp.cos(ang_a), jnp.sin(ang_a)
    tab_a = jnp.concatenate([cos_a, cos_a, -sin_a, sin_a], axis=-1)
    ident_a = jnp.concatenate([jnp.ones((T_CTX, HD), F32), jnp.zeros((T_CTX, HD), F32)], axis=-1)
    ang_c = angles(ROPE)
    cos_c, sin_c = jnp.cos(ang_c), jnp.sin(ang_c)
    half = ROPE // 2
    z_half = jnp.zeros((SEQ, half), F32)
    z_pad = jnp.zeros((SEQ, 128 - ROPE), F32)
    tab_c = jnp.concatenate([cos_c, cos_c, z_pad, -sin_c, z_half, z_pad, z_half, sin_c, z_pad], axis=-1)
    ident_c = jnp.concatenate([jnp.ones((T_CTX, ROPE), F32), jnp.zeros((T_CTX, 3 * 128 - ROPE), F32)], axis=-1)
    return jnp.concatenate([tab_a, ident_a], axis=0), jnp.concatenate([tab_c, ident_c], axis=0)


def _pad_head_cols(w, lead):
    w = w.reshape(lead, MLA_H, MLA_QK)
    return jnp.pad(w, ((0, 0), (0, 0), (0, MLA_PAD - MLA_QK))).reshape(lead, MLA_H * MLA_PAD)


def kernel(x, c, ctx, c_ctx, w_mod, b_mod, norm1, norm2, w_in, gqa_q_norm, gqa_k_norm, mla_q_lora_norm, w_uq,
           mla_kv_lora_norm, w_ukv, mla_q_norm, mla_k_norm, conv_w, w_out_a, w_out_b, w_out_c, b_gate, w_o,
           w_group, b_group, w_router, b_router, w_gu, w_down):
    xs = jnp.concatenate([x.reshape(T_LAT, D), ctx.reshape(T_CTX, D)], axis=0)
    cin = jnp.concatenate([c, c_ctx[None, :], jnp.zeros((3, D), F32)], axis=0)
    mod_all = _modulation(cin, w_mod, b_mod)
    rope_a, rope_c = _rope_tables()

    for l in range(DEPTH):
        last = l == DEPTH - 1
        n_rows = T_LAT if last else T_ALL
        mod = mod_all[l].reshape(8, 1, 6 * D)

        w_in_b = w_in[l].astype(BF16)
        w_kvq = jnp.concatenate(
            [w_in_b[:, :KV_COLS], jnp.zeros((D, 1024 - KV_COLS), BF16), w_in_b[:, KV_COLS:REST_OFF]], axis=1)
        w_rest = w_in_b[:, REST_OFF:]

        pkvq = _normed_matmul(xs, norm1[l], mod, 1, 0, w_kvq, T_ALL, KVQ_PAD // 2, "in_proj_kvq")
        prest = _normed_matmul(xs, norm1[l], mod, 1, 0, w_rest, n_rows, 1536, "in_proj_rest")

        ka, va, kc, vc, qa, qc = _attention_prep(
            pkvq, rope_a, rope_c,
            gqa_q_norm[l].reshape(1, HD), gqa_k_norm[l].reshape(1, HD),
            mla_q_lora_norm[l].reshape(1, Q_LORA), mla_kv_lora_norm[l].reshape(1, KV_LORA),
            jnp.pad(mla_q_norm[l], (0, MLA_PAD - MLA_QK)).reshape(1, MLA_PAD),
            jnp.pad(mla_k_norm[l], (0, MLA_PAD - MLA_QK)).reshape(1, MLA_PAD),
            _pad_head_cols(w_uq[l], Q_LORA).astype(BF16), w_ukv[l].astype(BF16))

        y_a = _attention(qa, ka, va, heads=KVH, group=GROUP, dk=HD, dv=HD, chunk=512,
                         with_ctx_queries=not last, name="gqa_attention")
        y_c = _attention(qc, kc, vc, heads=MLA_H, group=1, dk=MLA_PAD, dv=MLA_V, chunk=1024,
                         with_ctx_queries=not last, name="mla_attention")
        y_b = _conv_branch(prest, conv_w[l], n_rows)
        m = _merge(y_a, y_b, y_c, prest, b_gate[l].reshape(1, 3 * D),
                   w_out_a[l].astype(BF16), w_out_b[l].astype(BF16), w_out_c[l].astype(BF16), n_rows)
        xs = _oproj_residual(m, w_o[l].astype(BF16), xs, mod, 2, n_rows)

        w_r = jnp.concatenate([w_group[l], w_router[l], jnp.zeros((D, ROUTE_PAD - N_GROUPS - N_EXPERTS), F32)], axis=1)
        w_r_hi = w_r.astype(BF16)
        w_r_lo = (w_r - w_r_hi.astype(F32)).astype(BF16)
        b_r = jnp.concatenate([b_group[l], b_router[l], jnp.zeros((ROUTE_PAD - N_GROUPS - N_EXPERTS,), F32)])
        h2, logits = _router(xs, norm2[l], mod, 4, 3, w_r_hi, w_r_lo, b_r.reshape(1, ROUTE_PAD), n_rows)
        block_expert, slot_tok, slot_w, pos, n_used, n_blocks = _route(logits, n_rows)
        ys = _experts(block_expert, slot_tok, n_used, h2, w_gu[l].astype(BF16), w_down[l].astype(BF16),
                      slot_w, n_blocks)
        xs = _combine(pos, ys, xs, mod, 5, n_rows)

    return xs[:T_LAT].reshape(BATCH, SEQ, D)
```

```python
import functools

import numpy as np
import jax
import jax.numpy as jnp
from jax import lax
from jax.experimental import pallas as pl
from jax.experimental.pallas import tpu as pltpu

D = 2048
BATCH = 4
SEQ = 4096
DEPTH = 2
GRID_W = 64
CTX = 256
ROPE_THETA = 10000.0
EPS = 1e-6
KVH = 2
GROUP = 4
HD = 128
CONV_DIM = 1024
MLA_H = 8
Q_LORA = 512
KV_LORA = 256
NOPE = 128
ROPE = 64
MLA_V = 128
MLA_QK = NOPE + ROPE
MLA_PAD = 256
N_GROUPS = 4
EPG = 8
N_EXPERTS = N_GROUPS * EPG
TOP_K = 2
D_EXPERT = 512
MOE_BLOCK = 256

T_LAT = BATCH * SEQ
T_CTX = BATCH * CTX
T_ALL = T_LAT + T_CTX
KV_COLS = 2 * KVH * HD + KV_LORA + ROPE
Q_COLS = KVH * GROUP * HD + Q_LORA
REST_OFF = KV_COLS + Q_COLS
KVQ_PAD = 1024 + Q_COLS
REST_COLS = 3 * CONV_DIM + 3 * D
ROUTE_PAD = 128

V7X_VMEM_LIMIT = 56 * 1024 * 1024
TM = 1024
ATTN_ROWS = 1024
BF16_SUBLANES = 16
LOG2_E = float(np.log2(np.e))

F32 = jnp.float32
BF16 = jnp.bfloat16


def _params(*sem):
    return pltpu.CompilerParams(dimension_semantics=sem, vmem_limit_bytes=V7X_VMEM_LIMIT)


def _dot(a, b):
    return jnp.dot(a, b, preferred_element_type=F32)


def _mod_kernel(c_ref, w_ref, b_ref, o_ref):
    c = c_ref[...]
    a = (c * jax.nn.sigmoid(c)).astype(BF16)
    o_ref[...] = _dot(a, w_ref[...].astype(BF16)) + b_ref[...]


def _modulation(cin, w_mod, b_mod):
    tn = 1024
    return pl.pallas_call(
        _mod_kernel,
        out_shape=jax.ShapeDtypeStruct((DEPTH, 8, 6 * D), F32),
        grid=(DEPTH, 6 * D // tn),
        in_specs=[
            pl.BlockSpec((8, D), lambda l, j: (0, 0)),
            pl.BlockSpec((None, D, tn), lambda l, j: (l, 0, j)),
            pl.BlockSpec((None, 1, tn), lambda l, j: (l, 0, j)),
        ],
        out_specs=pl.BlockSpec((None, 8, tn), lambda l, j: (l, 0, j)),
        compiler_params=_params("parallel", "parallel"),
        name="modulation",
    )(cin, w_mod, b_mod.reshape(DEPTH, 1, 6 * D))


def _norm_mod_rows(x, g, sc, sh):
    ms = jnp.mean(x * x, axis=-1, keepdims=True)
    return (x * lax.rsqrt(ms + EPS) * g) * (1.0 + sc) + sh


def _normed_matmul_kernel(x_ref, g_ref, sc_ref, sh_ref, w_ref, o_ref, h_scr):
    rows = 128

    @pl.when(pl.program_id(1) == 0)
    def _():
        def body(r, carry):
            sl = pl.ds(pl.multiple_of(r * rows, rows), rows)
            h_scr[sl, :] = _norm_mod_rows(x_ref[sl, :], g_ref[...], sc_ref[...], sh_ref[...]).astype(BF16)
            return carry

        lax.fori_loop(0, TM // rows, body, 0)

    o_ref[...] = _dot(h_scr[...], w_ref[...]).astype(o_ref.dtype)


def _normed_matmul(xs, gain, mod, sc_chunk, sh_chunk, w, n_rows, tn, name):
    n = w.shape[1]
    return pl.pallas_call(
        _normed_matmul_kernel,
        out_shape=jax.ShapeDtypeStruct((n_rows, n), BF16),
        grid=(n_rows // TM, n // tn),
        in_specs=[
            pl.BlockSpec((TM, D), lambda i, j: (i, 0)),
            pl.BlockSpec((1, D), lambda i, j: (0, 0)),
            pl.BlockSpec((None, 1, D), lambda i, j: (i // (SEQ // TM), 0, sc_chunk)),
            pl.BlockSpec((None, 1, D), lambda i, j: (i // (SEQ // TM), 0, sh_chunk)),
            pl.BlockSpec((D, tn), lambda i, j: (0, j)),
        ],
        out_specs=pl.BlockSpec((TM, tn), lambda i, j: (i, j)),
        scratch_shapes=[pltpu.VMEM((TM, D), BF16)],
        compiler_params=_params("parallel", "arbitrary"),
        name=name,
    )(xs, gain.reshape(1, D), mod, mod, w)


def _rms_lanes(x, width):
    return lax.rsqrt(jnp.sum(x * x, axis=-1, keepdims=True) * (1.0 / width) + EPS)


def _prep_kernel(p_ref, ra_ref, rc_ref, gq_ref, gk_ref, gql_ref, gkl_ref, gmq_ref, gmk_ref, wuq_ref, wukv_ref,
                 ka_ref, va_ref, kc_ref, vc_ref, qa_ref, qc_ref):
    cos_a = ra_ref[:, 0:HD]
    sin_a = ra_ref[:, HD:2 * HD]
    cos_c = rc_ref[:, 0:128]
    sin_up = rc_ref[:, 128:256]
    sin_dn = rc_ref[:, 256:384]

    def rope_a(x):
        return x * cos_a + pltpu.roll(x, HD // 2, 1) * sin_a

    def rope_c(x):
        return x * cos_c + pltpu.roll(x, 128 - ROPE // 2, 1) * sin_up + pltpu.roll(x, ROPE // 2, 1) * sin_dn

    for h in range(KVH):
        k = p_ref[:, h * HD:(h + 1) * HD].astype(F32)
        ka_ref[:, h * HD:(h + 1) * HD] = rope_a(k * _rms_lanes(k, HD) * gk_ref[...]).astype(BF16)
    ones = jnp.ones((p_ref.shape[0], 128), BF16)
    for h in range(KVH):
        va_ref[:, 2 * h * HD:(2 * h + 1) * HD] = p_ref[:, (KVH + h) * HD:(KVH + h + 1) * HD]
        va_ref[:, (2 * h + 1) * HD:(2 * h + 2) * HD] = ones

    ckv = p_ref[:, 512:768].astype(F32)
    ckv_n = (ckv * _rms_lanes(ckv, KV_LORA) * gkl_ref[...]).astype(BF16)
    kv = _dot(ckv_n, wukv_ref[...])
    krope = p_ref[:, 768:896].astype(F32)
    krope_ss = jnp.sum(krope * krope, axis=-1, keepdims=True)
    krope_rot = rope_c(krope * gmk_ref[:, 128:256])
    for h in range(MLA_H):
        kn = kv[:, h * 256:h * 256 + NOPE]
        r = lax.rsqrt((jnp.sum(kn * kn, axis=-1, keepdims=True) + krope_ss) * (1.0 / MLA_QK) + EPS)
        kc_ref[:, h * MLA_PAD:h * MLA_PAD + NOPE] = (kn * r * gmk_ref[:, 0:128]).astype(BF16)
        kc_ref[:, h * MLA_PAD + NOPE:(h + 1) * MLA_PAD] = (krope_rot * r).astype(BF16)
        vc_ref[:, 2 * h * MLA_V:(2 * h + 1) * MLA_V] = kv[:, h * 256 + NOPE:(h + 1) * 256].astype(BF16)
        vc_ref[:, (2 * h + 1) * MLA_V:(2 * h + 2) * MLA_V] = ones

    for h in range(KVH * GROUP):
        q = p_ref[:, 1024 + h * HD:1024 + (h + 1) * HD].astype(F32)
        q = rope_a(q * _rms_lanes(q, HD) * gq_ref[...]) * (HD ** -0.5 * LOG2_E)
        qa_ref[:, h * HD:(h + 1) * HD] = q.astype(BF16)

    cq = p_ref[:, 2048:2048 + Q_LORA].astype(F32)
    cq_n = (cq * _rms_lanes(cq, Q_LORA) * gql_ref[...]).astype(BF16)
    qc = _dot(cq_n, wuq_ref[...])
    for h in range(MLA_H):
        qn = qc[:, h * MLA_PAD:h * MLA_PAD + NOPE]
        qr = qc[:, h * MLA_PAD + NOPE:(h + 1) * MLA_PAD]
        r = lax.rsqrt((jnp.sum(qn * qn, axis=-1, keepdims=True) + jnp.sum(qr * qr, axis=-1, keepdims=True))
                      * (1.0 / MLA_QK) + EPS) * (MLA_QK ** -0.5 * LOG2_E)
        qc_ref[:, h * MLA_PAD:h * MLA_PAD + NOPE] = (qn * r * gmq_ref[:, 0:128]).astype(BF16)
        qc_ref[:, h * MLA_PAD + NOPE:(h + 1) * MLA_PAD] = (rope_c(qr * gmq_ref[:, 128:256]) * r).astype(BF16)


def _attention_prep(pkvq, rope_a, rope_c, gq, gk, gql, gkl, gmq, gmk, wuq, wukv):
    tm = 256
    n_lat = T_LAT // tm
    per_seq = SEQ // tm

    def rope_idx(i):
        return jnp.where(i < n_lat, i % per_seq, per_seq + i - n_lat)

    def full(shape):
        return pl.BlockSpec(shape, lambda i: (0, 0))

    def rows(width):
        return pl.BlockSpec((tm, width), lambda i: (i, 0))

    out_widths = (KVH * HD, 2 * KVH * HD, MLA_H * MLA_PAD, 2 * MLA_H * MLA_V, KVH * GROUP * HD, MLA_H * MLA_PAD)
    return pl.pallas_call(
        _prep_kernel,
        out_shape=[jax.ShapeDtypeStruct((T_ALL, w), BF16) for w in out_widths],
        grid=(T_ALL // tm,),
        in_specs=[
            rows(KVQ_PAD),
            pl.BlockSpec((tm, 2 * HD), lambda i: (rope_idx(i), 0)),
            pl.BlockSpec((tm, 3 * 128), lambda i: (rope_idx(i), 0)),
            full((1, HD)), full((1, HD)), full((1, Q_LORA)), full((1, KV_LORA)),
            full((1, MLA_PAD)), full((1, MLA_PAD)),
            full((Q_LORA, MLA_H * MLA_PAD)), full((KV_LORA, MLA_H * (NOPE + MLA_V))),
        ],
        out_specs=[rows(w) for w in out_widths],
        compiler_params=_params("parallel"),
        name="attention_prep",
    )(pkvq, rope_a, rope_c, gq, gk, gql, gkl, gmq, gmk, wuq, wukv)


def _attn_kernel(q_ref, *refs, group, dk, dv, chunk):
    segments = [(refs[2 * i], refs[2 * i + 1]) for i in range((len(refs) - 1) // 2)]
    o_ref = refs[-1]
    tq = q_ref.shape[0]
    if group > 1:
        q = jnp.concatenate([q_ref[:, g * dk:(g + 1) * dk] for g in range(group)], axis=0)
    else:
        q = q_ref[...]
    m = None
    for k_ref, v_ref in segments:
        n_keys = k_ref.shape[0]
        step = min(chunk, n_keys)
        for c0 in range(0, n_keys, step):
            s = lax.dot_general(q, k_ref[c0:c0 + step, :], (((1,), (1,)), ((), ())), preferred_element_type=F32)
            c_max = jnp.max(s, axis=-1, keepdims=True)
            if m is None:
                m = c_max
                acc = _dot(jnp.exp2(s - m).astype(BF16), v_ref[c0:c0 + step, :])
            else:
                m_new = jnp.maximum(m, c_max)
                acc = jnp.exp2(m - m_new) * acc + _dot(jnp.exp2(s - m_new).astype(BF16), v_ref[c0:c0 + step, :])
                m = m_new
    o = acc[:, :dv] / acc[:, dv:]
    for g in range(group):
        o_ref[:, g * dv:(g + 1) * dv] = o[g * tq:(g + 1) * tq].astype(o_ref.dtype)


def _attention_latent(q, k, v, *, heads, group, dk, dv, tq, chunk, name):
    n_qt = SEQ // tq
    kern = functools.partial(_attn_kernel, group=group, dk=dk, dv=dv, chunk=chunk)
    return pl.pallas_call(
        kern,
        out_shape=jax.ShapeDtypeStruct((T_LAT, heads * group * dv), BF16),
        grid=(BATCH, heads, n_qt),
        in_specs=[
            pl.BlockSpec((tq, group * dk), lambda b, h, qi: (b * n_qt + qi, h)),
            pl.BlockSpec((CTX, dk), lambda b, h, qi: (T_LAT // CTX + b, h)),
            pl.BlockSpec((CTX, 2 * dv), lambda b, h, qi: (T_LAT // CTX + b, h)),
            pl.BlockSpec((SEQ, dk), lambda b, h, qi: (b, h)),
            pl.BlockSpec((SEQ, 2 * dv), lambda b, h, qi: (b, h)),
        ],
        out_specs=pl.BlockSpec((tq, group * dv), lambda b, h, qi: (b * n_qt + qi, h)),
        compiler_params=_params("parallel", "parallel", "arbitrary"),
        name=name,
    )(q, k, v, k, v)


def _attention_context(q, k, v, *, heads, group, dk, dv, name):
    kern = functools.partial(_attn_kernel, group=group, dk=dk, dv=dv, chunk=CTX)
    ctx0 = T_LAT // CTX
    return pl.pallas_call(
        kern,
        out_shape=jax.ShapeDtypeStruct((T_CTX, heads * group * dv), BF16),
        grid=(BATCH, heads),
        in_specs=[
            pl.BlockSpec((CTX, group * dk), lambda b, h: (ctx0 + b, h)),
            pl.BlockSpec((CTX, dk), lambda b, h: (ctx0 + b, h)),
            pl.BlockSpec((CTX, 2 * dv), lambda b, h: (ctx0 + b, h)),
        ],
        out_specs=pl.BlockSpec((CTX, group * dv), lambda b, h: (b, h)),
        compiler_params=_params("parallel", "parallel"),
        name=name,
    )(q, k, v)


def _attention(q, k, v, *, heads, group, dk, dv, tq, chunk, with_ctx_queries, name):
    y = _attention_latent(q, k, v, heads=heads, group=group, dk=dk, dv=dv, tq=tq, chunk=chunk, name=name)
    if with_ctx_queries:
        y_ctx = _attention_context(q, k, v, heads=heads, group=group, dk=dk, dv=dv, name=name + "_ctx")
        y = jnp.concatenate([y, y_ctx], axis=0)
    return y


def _conv_kernel(xb_ref, gb_ref, gc_ref, xbp_ref, gcp_ref, xbn_ref, gcn_ref, w_ref, o_ref):
    i = pl.program_id(0)
    z = gc_ref[...].astype(F32) * xb_ref[...].astype(F32)
    last = BF16_SUBLANES - 1
    z_prev = gcp_ref[last:last + 1, :].astype(F32) * xbp_ref[last:last + 1, :].astype(F32)
    z_next = gcn_ref[0:1, :].astype(F32) * xbn_ref[0:1, :].astype(F32)
    row = lax.broadcasted_iota(jnp.int32, (TM, 1), 0)
    seq_len = jnp.where(i < T_LAT // TM, SEQ, CTX)
    pos = (i * TM + row) & (seq_len - 1)
    z_dn = jnp.where(row == 0, z_prev, pltpu.roll(z, 1, 0))
    z_up = jnp.where(row == TM - 1, z_next, pltpu.roll(z, TM - 1, 0))
    z_dn = jnp.where(pos == 0, 0.0, z_dn)
    z_up = jnp.where(pos == seq_len - 1, 0.0, z_up)
    y = gb_ref[...].astype(F32) * (w_ref[0:1, :] * z_dn + w_ref[1:2, :] * z + w_ref[2:3, :] * z_up)
    o_ref[...] = y.astype(BF16)


def _conv_branch(prest, conv_w, n_rows):
    tc = 256
    n_ct = CONV_DIM // tc
    halo = TM // BF16_SUBLANES
    last_halo = n_rows // BF16_SUBLANES - 1

    def main(col0):
        return pl.BlockSpec((TM, tc), lambda i, j: (i, col0 * n_ct + j))

    def prev(col0):
        return pl.BlockSpec((BF16_SUBLANES, tc), lambda i, j: (jnp.maximum(i * halo - 1, 0), col0 * n_ct + j))

    def nxt(col0):
        return pl.BlockSpec((BF16_SUBLANES, tc), lambda i, j: (jnp.minimum((i + 1) * halo, last_halo), col0 * n_ct + j))

    return pl.pallas_call(
        _conv_kernel,
        out_shape=jax.ShapeDtypeStruct((n_rows, CONV_DIM), BF16),
        grid=(n_rows // TM, n_ct),
        in_specs=[main(0), main(1), main(2), prev(0), prev(2), nxt(0), nxt(2),
                  pl.BlockSpec((3, tc), lambda i, j: (0, j))],
        out_specs=pl.BlockSpec((TM, tc), lambda i, j: (i, j)),
        compiler_params=_params("parallel", "parallel"),
        name="conv_branch",
    )(prest, prest, prest, prest, prest, prest, prest, conv_w)


def _merge_kernel(ya_ref, yb_ref, yc_ref, ga_ref, gb_ref, gc_ref, ba_ref, bb_ref, bc_ref,
                  wa_ref, wb_ref, wc_ref, o_ref):
    def branch(y_ref, w_ref, g_ref, b_ref):
        return jax.nn.sigmoid(g_ref[...].astype(F32) + b_ref[...]) * _dot(y_ref[...], w_ref[...])

    m = branch(ya_ref, wa_ref, ga_ref, ba_ref)
    m = m + branch(yb_ref, wb_ref, gb_ref, bb_ref)
    m = m + branch(yc_ref, wc_ref, gc_ref, bc_ref)
    o_ref[...] = m.astype(BF16)


def _merge(ya, yb, yc, prest, b_gate, wa, wb, wc, n_rows):
    tn = 512
    n_ct = D // tn
    gate_blk0 = 3 * CONV_DIM // tn

    def y_spec():
        return pl.BlockSpec((TM, CONV_DIM), lambda i, j: (i, 0))

    def gate_spec(k):
        return pl.BlockSpec((TM, tn), lambda i, j: (i, gate_blk0 + k * n_ct + j))

    def bias_spec(k):
        return pl.BlockSpec((1, tn), lambda i, j: (0, k * n_ct + j))

    def w_spec():
        return pl.BlockSpec((CONV_DIM, tn), lambda i, j: (0, j))

    return pl.pallas_call(
        _merge_kernel,
        out_shape=jax.ShapeDtypeStruct((n_rows, D), BF16),
        grid=(n_rows // TM, n_ct),
        in_specs=[y_spec(), y_spec(), y_spec(), gate_spec(0), gate_spec(1), gate_spec(2),
                  bias_spec(0), bias_spec(1), bias_spec(2), w_spec(), w_spec(), w_spec()],
        out_specs=pl.BlockSpec((TM, tn), lambda i, j: (i, j)),
        compiler_params=_params("parallel", "arbitrary"),
        name="merge_branches",
    )(ya, yb, yc, prest, prest, prest, b_gate, b_gate, b_gate, wa, wb, wc)


def _oproj_kernel(m_ref, w_ref, x_ref, g_ref, o_ref):
    o_ref[...] = x_ref[...] + g_ref[...] * _dot(m_ref[...], w_ref[...])


def _oproj_residual(m, w_o, xs, mod, gate_chunk, n_rows):
    tn = 512
    n_ct = D // tn
    return pl.pallas_call(
        _oproj_kernel,
        out_shape=jax.ShapeDtypeStruct((n_rows, D), F32),
        grid=(n_rows // TM, n_ct),
        in_specs=[
            pl.BlockSpec((TM, D), lambda i, j: (i, 0)),
            pl.BlockSpec((D, tn), lambda i, j: (0, j)),
            pl.BlockSpec((TM, tn), lambda i, j: (i, j)),
            pl.BlockSpec((None, 1, tn), lambda i, j: (i // (SEQ // TM), 0, gate_chunk * n_ct + j)),
        ],
        out_specs=pl.BlockSpec((TM, tn), lambda i, j: (i, j)),
        compiler_params=_params("parallel", "arbitrary"),
        name="oproj_residual",
    )(m, w_o, xs, mod)


def _first_lane_of(mask, lane):
    return jnp.min(jnp.where(mask, lane, float(ROUTE_PAD)), axis=-1, keepdims=True)


def _router_kernel(x_ref, g_ref, sc_ref, sh_ref, whi_ref, wlo_ref, b_ref, h_ref, route_ref, cnt_ref, cnt_scr):
    rows = 128

    @pl.when(pl.program_id(0) == 0)
    def _():
        cnt_scr[...] = jnp.zeros_like(cnt_scr)

    lane = lax.broadcasted_iota(jnp.int32, (rows, ROUTE_PAD), 1).astype(F32)
    tri = (lax.broadcasted_iota(jnp.int32, (rows, rows), 0)
           >= lax.broadcasted_iota(jnp.int32, (rows, rows), 1)).astype(BF16)
    neg = -jnp.inf

    def body(r, carry):
        sl = pl.ds(pl.multiple_of(r * rows, rows), rows)
        h = _norm_mod_rows(x_ref[sl, :], g_ref[...], sc_ref[...], sh_ref[...])
        h_ref[sl, :] = h
        h_hi = h.astype(BF16)
        h_lo = (h - h_hi.astype(F32)).astype(BF16)
        lg = _dot(h_hi, whi_ref[...]) + (_dot(h_hi, wlo_ref[...]) + _dot(h_lo, whi_ref[...])) + b_ref[...]

        is_group = lane < N_GROUPS
        g_logit = jnp.where(is_group, lg, neg)
        g_max = jnp.max(g_logit, axis=-1, keepdims=True)
        g_prob = 1.0 / jnp.sum(jnp.where(is_group, jnp.exp(g_logit - g_max), 0.0), axis=-1, keepdims=True)
        g_idx = _first_lane_of(g_logit == g_max, lane)

        lo = N_GROUPS + g_idx * EPG
        cand = jnp.where((lane >= lo) & (lane < lo + EPG), lg, neg)
        m0 = jnp.max(cand, axis=-1, keepdims=True)
        i0 = _first_lane_of(cand == m0, lane)
        cand = jnp.where(lane == i0, neg, cand)
        m1 = jnp.max(cand, axis=-1, keepdims=True)
        i1 = _first_lane_of(cand == m1, lane)
        t = jnp.exp(m1 - m0)
        w0 = g_prob / (1.0 + t)
        w1 = g_prob * t / (1.0 + t)

        hit0 = lane == i0
        hit1 = lane == i1
        hits = jnp.where(hit0 | hit1, 1.0, 0.0)
        prefix = _dot(tri, hits.astype(BF16))
        before = prefix - hits + cnt_scr[...]
        rank0 = jnp.sum(jnp.where(hit0, before, 0.0), axis=-1, keepdims=True)
        rank1 = jnp.sum(jnp.where(hit1, before, 0.0), axis=-1, keepdims=True)
        cnt_scr[...] = cnt_scr[...] + prefix[rows - 1:rows, :]

        fields = (i0 - N_GROUPS, i1 - N_GROUPS, w0, w1, rank0, rank1)
        route = jnp.zeros((rows, ROUTE_PAD), F32)
        for k, val in enumerate(fields):
            route = jnp.where(lane == k, val, route)
        route_ref[sl, :] = route
        return carry

    lax.fori_loop(0, TM // rows, body, 0)
    cnt_ref[...] = cnt_scr[...]


def _router(xs, gain, mod, sc_chunk, sh_chunk, w_hi, w_lo, bias, n_rows):
    return pl.pallas_call(
        _router_kernel,
        out_shape=[jax.ShapeDtypeStruct((n_rows, D), F32), jax.ShapeDtypeStruct((n_rows, ROUTE_PAD), F32),
                   jax.ShapeDtypeStruct((1, ROUTE_PAD), F32)],
        grid=(n_rows // TM,),
        in_specs=[
            pl.BlockSpec((TM, D), lambda i: (i, 0)),
            pl.BlockSpec((1, D), lambda i: (0, 0)),
            pl.BlockSpec((None, 1, D), lambda i: (i // (SEQ // TM), 0, sc_chunk)),
            pl.BlockSpec((None, 1, D), lambda i: (i // (SEQ // TM), 0, sh_chunk)),
            pl.BlockSpec((D, ROUTE_PAD), lambda i: (0, 0)),
            pl.BlockSpec((D, ROUTE_PAD), lambda i: (0, 0)),
            pl.BlockSpec((1, ROUTE_PAD), lambda i: (0, 0)),
        ],
        out_specs=[pl.BlockSpec((TM, D), lambda i: (i, 0)), pl.BlockSpec((TM, ROUTE_PAD), lambda i: (i, 0)),
                   pl.BlockSpec((1, ROUTE_PAD), lambda i: (0, 0))],
        scratch_shapes=[pltpu.VMEM((1, ROUTE_PAD), F32)],
        compiler_params=_params("arbitrary"),
        name="moe_router",
    )(xs, gain.reshape(1, D), mod, mod, w_hi, w_lo, bias)


def _row_copy(src_hbm, src_row, dst, dst_row, sem):
    return pltpu.make_async_copy(src_hbm.at[pl.ds(src_row, 1), :], dst.at[pl.ds(dst_row, 1), :], sem)


def _start_row_gather(src_hbm, dst, sem, n, index_of):
    def issue(r, carry):
        _row_copy(src_hbm, index_of(r), dst, r, sem).start()
        return carry

    lax.fori_loop(0, n, issue, 0, unroll=8)


def _wait_row_gather(src_hbm, dst, sem, n):
    def drain(r, carry):
        _row_copy(src_hbm, 0, dst, r, sem).wait()
        return carry

    lax.fori_loop(0, n, drain, 0, unroll=8)


def _expert_kernel(be_ref, tok_ref, nused_ref, h_hbm, wgu_ref, wdn_ref, sw_ref, o_ref, xbuf, sem):
    i = pl.program_id(0)
    n_used = nused_ref[0]
    slot = i & 1

    def start(blk, s):
        _start_row_gather(h_hbm, xbuf.at[s], sem.at[s], MOE_BLOCK, lambda r: tok_ref[blk * MOE_BLOCK + r])

    @pl.when((i == 0) & (n_used > 0))
    def _():
        start(0, 0)

    @pl.when(i < n_used)
    def _():
        _wait_row_gather(h_hbm, xbuf.at[slot], sem.at[slot], MOE_BLOCK)

        @pl.when(i + 1 < n_used)
        def _():
            start(i + 1, 1 - slot)

        gu = _dot(xbuf[slot].astype(BF16), wgu_ref[...])
        gate = gu[:, :D_EXPERT]
        act = gate * jax.nn.sigmoid(gate) * gu[:, D_EXPERT:]
        o_ref[...] = _dot(act.astype(BF16), wdn_ref[...]) * sw_ref[...]

    @pl.when(i >= n_used)
    def _():
        o_ref[...] = jnp.zeros_like(o_ref)


def _experts(block_expert, slot_tok, n_used, h2, w_gu, w_down, slot_w, n_blocks):
    grid_spec = pltpu.PrefetchScalarGridSpec(
        num_scalar_prefetch=3,
        grid=(n_blocks,),
        in_specs=[
            pl.BlockSpec(memory_space=pl.ANY),
            pl.BlockSpec((None, D, 2 * D_EXPERT), lambda i, be, tok, nu: (be[i], 0, 0)),
            pl.BlockSpec((None, D_EXPERT, D), lambda i, be, tok, nu: (be[i], 0, 0)),
            pl.BlockSpec((MOE_BLOCK, 1), lambda i, be, tok, nu: (i, 0)),
        ],
        out_specs=pl.BlockSpec((MOE_BLOCK, D), lambda i, be, tok, nu: (i, 0)),
        scratch_shapes=[pltpu.VMEM((2, MOE_BLOCK, D), F32), pltpu.SemaphoreType.DMA((2,))],
    )
    return pl.pallas_call(
        _expert_kernel,
        out_shape=jax.ShapeDtypeStruct((n_blocks * MOE_BLOCK, D), F32),
        grid_spec=grid_spec,
        compiler_params=_params("arbitrary"),
        name="moe_experts",
    )(block_expert, slot_tok, n_used, h2, w_gu, w_down, slot_w)


def _combine_kernel(pos_ref, ys_hbm, x_ref, g_ref, o_ref, buf, sem):
    i = pl.program_id(0)
    n_tiles = pl.num_programs(0)
    rows = x_ref.shape[0]
    slot = i & 1

    def start(tile, s):
        _start_row_gather(ys_hbm, buf.at[s], sem.at[s], TOP_K * rows,
                          lambda r: pos_ref[(tile * rows + (r & (rows - 1))) * TOP_K + (r >> int(np.log2(rows)))])

    @pl.when(i == 0)
    def _():
        start(0, 0)

    _wait_row_gather(ys_hbm, buf.at[slot], sem.at[slot], TOP_K * rows)

    @pl.when(i + 1 < n_tiles)
    def _():
        start(i + 1, 1 - slot)

    o_ref[...] = x_ref[...] + g_ref[...] * (buf[slot, 0:rows, :] + buf[slot, rows:2 * rows, :])


def _combine(pos, ys, xs, mod, gate_chunk, n_rows):
    rows = 256
    grid_spec = pltpu.PrefetchScalarGridSpec(
        num_scalar_prefetch=1,
        grid=(n_rows // rows,),
        in_specs=[
            pl.BlockSpec(memory_space=pl.ANY),
            pl.BlockSpec((rows, D), lambda i, pos: (i, 0)),
            pl.BlockSpec((None, 1, D), lambda i, pos: (i // (SEQ // rows), 0, gate_chunk)),
        ],
        out_specs=pl.BlockSpec((rows, D), lambda i, pos: (i, 0)),
        scratch_shapes=[pltpu.VMEM((2, TOP_K * rows, D), F32), pltpu.SemaphoreType.DMA((2,))],
    )
    return pl.pallas_call(
        _combine_kernel,
        out_shape=jax.ShapeDtypeStruct((n_rows, D), F32),
        grid_spec=grid_spec,
        compiler_params=_params("arbitrary"),
        name="moe_combine",
    )(pos, ys, xs, mod)


def _slot_layout(route, counts, n_tok):
    n_assign = n_tok * TOP_K
    expert = route[:, 0:TOP_K].astype(jnp.int32)
    weight = route[:, TOP_K:2 * TOP_K]
    rank = route[:, 2 * TOP_K:3 * TOP_K].astype(jnp.int32)
    sizes = counts[0, N_GROUPS:N_GROUPS + N_EXPERTS].astype(jnp.int32)
    padded = (sizes + MOE_BLOCK - 1) // MOE_BLOCK * MOE_BLOCK
    pad_end = jnp.cumsum(padded)
    pad_start = pad_end - padded
    pos = (pad_start[expert] + rank).reshape(n_assign)
    n_slots = (n_assign + MOE_BLOCK - 1) // MOE_BLOCK * MOE_BLOCK + N_EXPERTS * MOE_BLOCK
    n_blocks = n_slots // MOE_BLOCK
    slot_assign = jnp.full((n_slots,), n_assign, jnp.int32).at[pos].set(
        jnp.arange(n_assign, dtype=jnp.int32), unique_indices=True)
    is_real = slot_assign < n_assign
    slot_tok = jnp.where(is_real, slot_assign // TOP_K, 0)
    weight_ext = jnp.concatenate([weight.reshape(n_assign), jnp.zeros((1,), F32)])
    slot_w = weight_ext[slot_assign]
    block_start = jnp.arange(n_blocks, dtype=jnp.int32) * MOE_BLOCK
    block_expert = jnp.minimum(jnp.sum((pad_end[None, :] <= block_start[:, None]).astype(jnp.int32), axis=1),
                               N_EXPERTS - 1)
    n_used = (pad_end[-1] // MOE_BLOCK).reshape(1)
    return block_expert, slot_tok, slot_w.reshape(n_slots, 1), pos, n_used, n_blocks


def _rope_tables():
    s = jnp.arange(SEQ)
    rows = (s // GRID_W).astype(F32)
    cols = (s % GRID_W).astype(F32)

    def angles(rot_dim):
        n = rot_dim // 4
        inv = ROPE_THETA ** (-jnp.arange(n, dtype=F32) / n)
        return jnp.concatenate([rows[:, None] * inv, cols[:, None] * inv], axis=-1)

    ang_a = angles(HD)
    cos_a, sin_a = jnp.cos(ang_a), jnp.sin(ang_a)
    tab_a = jnp.concatenate([cos_a, cos_a, -sin_a, sin_a], axis=-1)
    ident_a = jnp.concatenate([jnp.ones((T_CTX, HD), F32), jnp.zeros((T_CTX, HD), F32)], axis=-1)
    ang_c = angles(ROPE)
    cos_c, sin_c = jnp.cos(ang_c), jnp.sin(ang_c)
    half = ROPE // 2
    z_half = jnp.zeros((SEQ, half), F32)
    z_pad = jnp.zeros((SEQ, 128 - ROPE), F32)
    tab_c = jnp.concatenate([cos_c, cos_c, z_pad, -sin_c, z_half, z_pad, z_half, sin_c, z_pad], axis=-1)
    ident_c = jnp.concatenate([jnp.ones((T_CTX, ROPE), F32), jnp.zeros((T_CTX, 3 * 128 - ROPE), F32)], axis=-1)
    return jnp.concatenate([tab_a, ident_a], axis=0), jnp.concatenate([tab_c, ident_c], axis=0)


def _pad_head_cols(w, lead):
    w = w.reshape(lead, MLA_H, MLA_QK)
    return jnp.pad(w, ((0, 0), (0, 0), (0, MLA_PAD - MLA_QK))).reshape(lead, MLA_H * MLA_PAD)


def kernel(x, c, ctx, c_ctx, w_mod, b_mod, norm1, norm2, w_in, gqa_q_norm, gqa_k_norm, mla_q_lora_norm, w_uq,
           mla_kv_lora_norm, w_ukv, mla_q_norm, mla_k_norm, conv_w, w_out_a, w_out_b, w_out_c, b_gate, w_o,
           w_group, b_group, w_router, b_router, w_gu, w_down):
    xs = jnp.concatenate([x.reshape(T_LAT, D), ctx.reshape(T_CTX, D)], axis=0)
    cin = jnp.concatenate([c, c_ctx[None, :], jnp.zeros((3, D), F32)], axis=0)
    mod_all = _modulation(cin, w_mod, b_mod)
    rope_a, rope_c = _rope_tables()

    for l in range(DEPTH):
        last = l == DEPTH - 1
        n_rows = T_LAT if last else T_ALL
        mod = mod_all[l].reshape(8, 1, 6 * D)

        w_in_b = w_in[l].astype(BF16)
        w_kvq = jnp.concatenate(
            [w_in_b[:, :KV_COLS], jnp.zeros((D, 1024 - KV_COLS), BF16), w_in_b[:, KV_COLS:REST_OFF]], axis=1)
        w_rest = w_in_b[:, REST_OFF:]

        pkvq = _normed_matmul(xs, norm1[l], mod, 1, 0, w_kvq, T_ALL, KVQ_PAD // 2, "in_proj_kvq")
        prest = _normed_matmul(xs, norm1[l], mod, 1, 0, w_rest, n_rows, 1536, "in_proj_rest")

        ka, va, kc, vc, qa, qc = _attention_prep(
            pkvq, rope_a, rope_c,
            gqa_q_norm[l].reshape(1, HD), gqa_k_norm[l].reshape(1, HD),
            mla_q_lora_norm[l].reshape(1, Q_LORA), mla_kv_lora_norm[l].reshape(1, KV_LORA),
            jnp.pad(mla_q_norm[l], (0, MLA_PAD - MLA_QK)).reshape(1, MLA_PAD),
            jnp.pad(mla_k_norm[l], (0, MLA_PAD - MLA_QK)).reshape(1, MLA_PAD),
            _pad_head_cols(w_uq[l], Q_LORA).astype(BF16), w_ukv[l].astype(BF16))

        y_a = _attention(qa, ka, va, heads=KVH, group=GROUP, dk=HD, dv=HD, tq=ATTN_ROWS // GROUP, chunk=512,
                         with_ctx_queries=not last, name="gqa_attention")
        y_c = _attention(qc, kc, vc, heads=MLA_H, group=1, dk=MLA_PAD, dv=MLA_V, tq=ATTN_ROWS, chunk=512,
                         with_ctx_queries=not last, name="mla_attention")
        y_b = _conv_branch(prest, conv_w[l], n_rows)
        m = _merge(y_a, y_b, y_c, prest, b_gate[l].reshape(1, 3 * D),
                   w_out_a[l].astype(BF16), w_out_b[l].astype(BF16), w_out_c[l].astype(BF16), n_rows)
        xs = _oproj_residual(m, w_o[l].astype(BF16), xs, mod, 2, n_rows)

        w_r = jnp.concatenate([w_group[l], w_router[l], jnp.zeros((D, ROUTE_PAD - N_GROUPS - N_EXPERTS), F32)], axis=1)
        w_r_hi = w_r.astype(BF16)
        w_r_lo = (w_r - w_r_hi.astype(F32)).astype(BF16)
        b_r = jnp.concatenate([b_group[l], b_router[l], jnp.zeros((ROUTE_PAD - N_GROUPS - N_EXPERTS,), F32)])
        h2, route, counts = _router(xs, norm2[l], mod, 4, 3, w_r_hi, w_r_lo, b_r.reshape(1, ROUTE_PAD), n_rows)
        block_expert, slot_tok, slot_w, pos, n_used, n_blocks = _slot_layout(route, counts, n_rows)
        ys = _experts(block_expert, slot_tok, n_used, h2, w_gu[l].astype(BF16), w_down[l].astype(BF16),
                      slot_w, n_blocks)
        xs = _combine(pos, ys, xs, mod, 5, n_rows)

    return xs[:T_LAT].reshape(BATCH, SEQ, D)
```

```python
import functools

import numpy as np
import jax
import jax.numpy as jnp
from jax import lax
from jax.experimental import pallas as pl
from jax.experimental.pallas import tpu as pltpu

D = 2048
BATCH = 4
SEQ = 4096
DEPTH = 2
GRID_W = 64
CTX = 256
ROPE_THETA = 10000.0
EPS = 1e-6
KVH = 2
GROUP = 4
HD = 128
CONV_DIM = 1024
MLA_H = 8
Q_LORA = 512
KV_LORA = 256
NOPE = 128
ROPE = 64
MLA_V = 128
MLA_QK = NOPE + ROPE
MLA_PAD = 256
N_GROUPS = 4
EPG = 8
N_EXPERTS = N_GROUPS * EPG
TOP_K = 2
D_EXPERT = 512
MOE_BLOCK = 256

T_LAT = BATCH * SEQ
T_CTX = BATCH * CTX
T_ALL = T_LAT + T_CTX
KV_COLS = 2 * KVH * HD + KV_LORA + ROPE
Q_COLS = KVH * GROUP * HD + Q_LORA
REST_OFF = KV_COLS + Q_COLS
KVQ_PAD = 1024 + Q_COLS
REST_COLS = 3 * CONV_DIM + 3 * D
ROUTE_PAD = 128

V7X_VMEM_LIMIT = 56 * 1024 * 1024
TM = 1024
ATTN_ROWS = 1024
BF16_SUBLANES = 16
LOG2_E = float(np.log2(np.e))

F32 = jnp.float32
BF16 = jnp.bfloat16


def _params(*sem):
    return pltpu.CompilerParams(dimension_semantics=sem, vmem_limit_bytes=V7X_VMEM_LIMIT)


def _dot(a, b):
    return jnp.dot(a, b, preferred_element_type=F32)


def _mod_kernel(c_ref, w_ref, b_ref, o_ref):
    c = c_ref[...]
    a = (c * jax.nn.sigmoid(c)).astype(BF16)
    o_ref[...] = _dot(a, w_ref[...].astype(BF16)) + b_ref[...]


def _modulation(cin, w_mod, b_mod):
    tn = 1024
    return pl.pallas_call(
        _mod_kernel,
        out_shape=jax.ShapeDtypeStruct((DEPTH, 8, 6 * D), F32),
        grid=(DEPTH, 6 * D // tn),
        in_specs=[
            pl.BlockSpec((8, D), lambda l, j: (0, 0)),
            pl.BlockSpec((None, D, tn), lambda l, j: (l, 0, j)),
            pl.BlockSpec((None, 1, tn), lambda l, j: (l, 0, j)),
        ],
        out_specs=pl.BlockSpec((None, 8, tn), lambda l, j: (l, 0, j)),
        compiler_params=_params("parallel", "parallel"),
        name="modulation",
    )(cin, w_mod, b_mod.reshape(DEPTH, 1, 6 * D))


def _norm_mod_rows(x, g, sc, sh):
    ms = jnp.mean(x * x, axis=-1, keepdims=True)
    return (x * lax.rsqrt(ms + EPS) * g) * (1.0 + sc) + sh


def _normed_matmul_kernel(x_ref, g_ref, sc_ref, sh_ref, w_ref, o_ref, h_scr):
    rows = 128

    @pl.when(pl.program_id(1) == 0)
    def _():
        def body(r, carry):
            sl = pl.ds(pl.multiple_of(r * rows, rows), rows)
            h_scr[sl, :] = _norm_mod_rows(x_ref[sl, :], g_ref[...], sc_ref[...], sh_ref[...]).astype(BF16)
            return carry

        lax.fori_loop(0, TM // rows, body, 0)

    o_ref[...] = _dot(h_scr[...], w_ref[...]).astype(o_ref.dtype)


def _normed_matmul(xs, gain, mod, sc_chunk, sh_chunk, w, n_rows, tn, name):
    n = w.shape[1]
    return pl.pallas_call(
        _normed_matmul_kernel,
        out_shape=jax.ShapeDtypeStruct((n_rows, n), BF16),
        grid=(n_rows // TM, n // tn),
        in_specs=[
            pl.BlockSpec((TM, D), lambda i, j: (i, 0)),
            pl.BlockSpec((1, D), lambda i, j: (0, 0)),
            pl.BlockSpec((None, 1, D), lambda i, j: (i // (SEQ // TM), 0, sc_chunk)),
            pl.BlockSpec((None, 1, D), lambda i, j: (i // (SEQ // TM), 0, sh_chunk)),
            pl.BlockSpec((D, tn), lambda i, j: (0, j)),
        ],
        out_specs=pl.BlockSpec((TM, tn), lambda i, j: (i, j)),
        scratch_shapes=[pltpu.VMEM((TM, D), BF16)],
        compiler_params=_params("parallel", "arbitrary"),
        name=name,
    )(xs, gain.reshape(1, D), mod, mod, w)


def _rms_lanes(x, width):
    return lax.rsqrt(jnp.sum(x * x, axis=-1, keepdims=True) * (1.0 / width) + EPS)


def _prep_kernel(p_ref, ra_ref, rc_ref, gq_ref, gk_ref, gql_ref, gkl_ref, gmq_ref, gmk_ref, wuq_ref, wukv_ref,
                 ka_ref, va_ref, kc_ref, vc_ref, qa_ref, qc_ref):
    cos_a = ra_ref[:, 0:HD]
    sin_a = ra_ref[:, HD:2 * HD]
    cos_c = rc_ref[:, 0:128]
    sin_up = rc_ref[:, 128:256]
    sin_dn = rc_ref[:, 256:384]

    def rope_a(x):
        return x * cos_a + pltpu.roll(x, HD // 2, 1) * sin_a

    def rope_c(x):
        return x * cos_c + pltpu.roll(x, 128 - ROPE // 2, 1) * sin_up + pltpu.roll(x, ROPE // 2, 1) * sin_dn

    for h in range(KVH):
        k = p_ref[:, h * HD:(h + 1) * HD].astype(F32)
        ka_ref[:, h * HD:(h + 1) * HD] = rope_a(k * _rms_lanes(k, HD) * gk_ref[...]).astype(BF16)
    ones = jnp.ones((p_ref.shape[0], 128), BF16)
    for h in range(KVH):
        va_ref[:, 2 * h * HD:(2 * h + 1) * HD] = p_ref[:, (KVH + h) * HD:(KVH + h + 1) * HD]
        va_ref[:, (2 * h + 1) * HD:(2 * h + 2) * HD] = ones

    ckv = p_ref[:, 512:768].astype(F32)
    ckv_n = (ckv * _rms_lanes(ckv, KV_LORA) * gkl_ref[...]).astype(BF16)
    kv = _dot(ckv_n, wukv_ref[...])
    krope = p_ref[:, 768:896].astype(F32)
    krope_ss = jnp.sum(krope * krope, axis=-1, keepdims=True)
    krope_rot = rope_c(krope * gmk_ref[:, 128:256])
    for h in range(MLA_H):
        kn = kv[:, h * 256:h * 256 + NOPE]
        r = lax.rsqrt((jnp.sum(kn * kn, axis=-1, keepdims=True) + krope_ss) * (1.0 / MLA_QK) + EPS)
        kc_ref[:, h * MLA_PAD:h * MLA_PAD + NOPE] = (kn * r * gmk_ref[:, 0:128]).astype(BF16)
        kc_ref[:, h * MLA_PAD + NOPE:(h + 1) * MLA_PAD] = (krope_rot * r).astype(BF16)
        vc_ref[:, 2 * h * MLA_V:(2 * h + 1) * MLA_V] = kv[:, h * 256 + NOPE:(h + 1) * 256].astype(BF16)
        vc_ref[:, (2 * h + 1) * MLA_V:(2 * h + 2) * MLA_V] = ones

    for h in range(KVH * GROUP):
        q = p_ref[:, 1024 + h * HD:1024 + (h + 1) * HD].astype(F32)
        q = rope_a(q * _rms_lanes(q, HD) * gq_ref[...]) * (HD ** -0.5 * LOG2_E)
        qa_ref[:, h * HD:(h + 1) * HD] = q.astype(BF16)

    cq = p_ref[:, 2048:2048 + Q_LORA].astype(F32)
    cq_n = (cq * _rms_lanes(cq, Q_LORA) * gql_ref[...]).astype(BF16)
    qc = _dot(cq_n, wuq_ref[...])
    for h in range(MLA_H):
        qn = qc[:, h * MLA_PAD:h * MLA_PAD + NOPE]
        qr = qc[:, h * MLA_PAD + NOPE:(h + 1) * MLA_PAD]
        r = lax.rsqrt((jnp.sum(qn * qn, axis=-1, keepdims=True) + jnp.sum(qr * qr, axis=-1, keepdims=True))
                      * (1.0 / MLA_QK) + EPS) * (MLA_QK ** -0.5 * LOG2_E)
        qc_ref[:, h * MLA_PAD:h * MLA_PAD + NOPE] = (qn * r * gmq_ref[:, 0:128]).astype(BF16)
        qc_ref[:, h * MLA_PAD + NOPE:(h + 1) * MLA_PAD] = (rope_c(qr * gmq_ref[:, 128:256]) * r).astype(BF16)


def _attention_prep(pkvq, rope_a, rope_c, gq, gk, gql, gkl, gmq, gmk, wuq, wukv):
    tm = 256
    n_lat = T_LAT // tm
    per_seq = SEQ // tm

    def rope_idx(i):
        return jnp.where(i < n_lat, i % per_seq, per_seq + i - n_lat)

    def full(shape):
        return pl.BlockSpec(shape, lambda i: (0, 0))

    def rows(width):
        return pl.BlockSpec((tm, width), lambda i: (i, 0))

    out_widths = (KVH * HD, 2 * KVH * HD, MLA_H * MLA_PAD, 2 * MLA_H * MLA_V, KVH * GROUP * HD, MLA_H * MLA_PAD)
    return pl.pallas_call(
        _prep_kernel,
        out_shape=[jax.ShapeDtypeStruct((T_ALL, w), BF16) for w in out_widths],
        grid=(T_ALL // tm,),
        in_specs=[
            rows(KVQ_PAD),
            pl.BlockSpec((tm, 2 * HD), lambda i: (rope_idx(i), 0)),
            pl.BlockSpec((tm, 3 * 128), lambda i: (rope_idx(i), 0)),
            full((1, HD)), full((1, HD)), full((1, Q_LORA)), full((1, KV_LORA)),
            full((1, MLA_PAD)), full((1, MLA_PAD)),
            full((Q_LORA, MLA_H * MLA_PAD)), full((KV_LORA, MLA_H * (NOPE + MLA_V))),
        ],
        out_specs=[rows(w) for w in out_widths],
        compiler_params=_params("parallel"),
        name="attention_prep",
    )(pkvq, rope_a, rope_c, gq, gk, gql, gkl, gmq, gmk, wuq, wukv)


def _attn_kernel(q_ref, *refs, group, dk, dv, chunk):
    segments = [(refs[2 * i], refs[2 * i + 1]) for i in range((len(refs) - 1) // 2)]
    o_ref = refs[-1]
    tq = q_ref.shape[0]
    if group > 1:
        q = jnp.concatenate([q_ref[:, g * dk:(g + 1) * dk] for g in range(group)], axis=0)
    else:
        q = q_ref[...]
    m = None
    for k_ref, v_ref in segments:
        n_keys = k_ref.shape[0]
        step = min(chunk, n_keys)
        for c0 in range(0, n_keys, step):
            s = lax.dot_general(q, k_ref[c0:c0 + step, :], (((1,), (1,)), ((), ())), preferred_element_type=F32)
            c_max = jnp.max(s, axis=-1, keepdims=True)
            if m is None:
                m = c_max
                acc = _dot(jnp.exp2(s - m).astype(BF16), v_ref[c0:c0 + step, :])
            else:
                m_new = jnp.maximum(m, c_max)
                acc = jnp.exp2(m - m_new) * acc + _dot(jnp.exp2(s - m_new).astype(BF16), v_ref[c0:c0 + step, :])
                m = m_new
    o = acc[:, :dv] / acc[:, dv:]
    for g in range(group):
        o_ref[:, g * dv:(g + 1) * dv] = o[g * tq:(g + 1) * tq].astype(o_ref.dtype)


def _attention_latent(q, k, v, *, heads, group, dk, dv, tq, chunk, name):
    n_qt = SEQ // tq
    kern = functools.partial(_attn_kernel, group=group, dk=dk, dv=dv, chunk=chunk)
    return pl.pallas_call(
        kern,
        out_shape=jax.ShapeDtypeStruct((T_LAT, heads * group * dv), BF16),
        grid=(BATCH, heads, n_qt),
        in_specs=[
            pl.BlockSpec((tq, group * dk), lambda b, h, qi: (b * n_qt + qi, h)),
            pl.BlockSpec((CTX, dk), lambda b, h, qi: (T_LAT // CTX + b, h)),
            pl.BlockSpec((CTX, 2 * dv), lambda b, h, qi: (T_LAT // CTX + b, h)),
            pl.BlockSpec((SEQ, dk), lambda b, h, qi: (b, h)),
            pl.BlockSpec((SEQ, 2 * dv), lambda b, h, qi: (b, h)),
        ],
        out_specs=pl.BlockSpec((tq, group * dv), lambda b, h, qi: (b * n_qt + qi, h)),
        compiler_params=_params("parallel", "parallel", "arbitrary"),
        name=name,
    )(q, k, v, k, v)


def _attention_context(q, k, v, *, heads, group, dk, dv, name):
    kern = functools.partial(_attn_kernel, group=group, dk=dk, dv=dv, chunk=CTX)
    ctx0 = T_LAT // CTX
    return pl.pallas_call(
        kern,
        out_shape=jax.ShapeDtypeStruct((T_CTX, heads * group * dv), BF16),
        grid=(BATCH, heads),
        in_specs=[
            pl.BlockSpec((CTX, group * dk), lambda b, h: (ctx0 + b, h)),
            pl.BlockSpec((CTX, dk), lambda b, h: (ctx0 + b, h)),
            pl.BlockSpec((CTX, 2 * dv), lambda b, h: (ctx0 + b, h)),
        ],
        out_specs=pl.BlockSpec((CTX, group * dv), lambda b, h: (b, h)),
        compiler_params=_params("parallel", "parallel"),
        name=name,
    )(q, k, v)


def _attention(q, k, v, *, heads, group, dk, dv, tq, chunk, with_ctx_queries, name):
    y = _attention_latent(q, k, v, heads=heads, group=group, dk=dk, dv=dv, tq=tq, chunk=chunk, name=name)
    if with_ctx_queries:
        y_ctx = _attention_context(q, k, v, heads=heads, group=group, dk=dk, dv=dv, name=name + "_ctx")
        y = jnp.concatenate([y, y_ctx], axis=0)
    return y


def _conv_kernel(xb_ref, gb_ref, gc_ref, xbp_ref, gcp_ref, xbn_ref, gcn_ref, w_ref, o_ref):
    i = pl.program_id(0)
    z = gc_ref[...].astype(F32) * xb_ref[...].astype(F32)
    last = BF16_SUBLANES - 1
    z_prev = gcp_ref[last:last + 1, :].astype(F32) * xbp_ref[last:last + 1, :].astype(F32)
    z_next = gcn_ref[0:1, :].astype(F32) * xbn_ref[0:1, :].astype(F32)
    row = lax.broadcasted_iota(jnp.int32, (TM, 1), 0)
    seq_len = jnp.where(i < T_LAT // TM, SEQ, CTX)
    pos = (i * TM + row) & (seq_len - 1)
    z_dn = jnp.where(row == 0, z_prev, pltpu.roll(z, 1, 0))
    z_up = jnp.where(row == TM - 1, z_next, pltpu.roll(z, TM - 1, 0))
    z_dn = jnp.where(pos == 0, 0.0, z_dn)
    z_up = jnp.where(pos == seq_len - 1, 0.0, z_up)
    y = gb_ref[...].astype(F32) * (w_ref[0:1, :] * z_dn + w_ref[1:2, :] * z + w_ref[2:3, :] * z_up)
    o_ref[...] = y.astype(BF16)


def _conv_branch(prest, conv_w, n_rows):
    tc = 256
    n_ct = CONV_DIM // tc
    halo = TM // BF16_SUBLANES
    last_halo = n_rows // BF16_SUBLANES - 1

    def main(col0):
        return pl.BlockSpec((TM, tc), lambda i, j: (i, col0 * n_ct + j))

    def prev(col0):
        return pl.BlockSpec((BF16_SUBLANES, tc), lambda i, j: (jnp.maximum(i * halo - 1, 0), col0 * n_ct + j))

    def nxt(col0):
        return pl.BlockSpec((BF16_SUBLANES, tc), lambda i, j: (jnp.minimum((i + 1) * halo, last_halo), col0 * n_ct + j))

    return pl.pallas_call(
        _conv_kernel,
        out_shape=jax.ShapeDtypeStruct((n_rows, CONV_DIM), BF16),
        grid=(n_rows // TM, n_ct),
        in_specs=[main(0), main(1), main(2), prev(0), prev(2), nxt(0), nxt(2),
                  pl.BlockSpec((3, tc), lambda i, j: (0, j))],
        out_specs=pl.BlockSpec((TM, tc), lambda i, j: (i, j)),
        compiler_params=_params("parallel", "parallel"),
        name="conv_branch",
    )(prest, prest, prest, prest, prest, prest, prest, conv_w)


def _merge_kernel(ya_ref, yb_ref, yc_ref, ga_ref, gb_ref, gc_ref, ba_ref, bb_ref, bc_ref,
                  wa_ref, wb_ref, wc_ref, o_ref):
    def branch(y_ref, w_ref, g_ref, b_ref):
        return jax.nn.sigmoid(g_ref[...].astype(F32) + b_ref[...]) * _dot(y_ref[...], w_ref[...])

    m = branch(ya_ref, wa_ref, ga_ref, ba_ref)
    m = m + branch(yb_ref, wb_ref, gb_ref, bb_ref)
    m = m + branch(yc_ref, wc_ref, gc_ref, bc_ref)
    o_ref[...] = m.astype(BF16)


def _merge(ya, yb, yc, prest, b_gate, wa, wb, wc, n_rows):
    tn = 512
    n_ct = D // tn
    gate_blk0 = 3 * CONV_DIM // tn

    def y_spec():
        return pl.BlockSpec((TM, CONV_DIM), lambda i, j: (i, 0))

    def gate_spec(k):
        return pl.BlockSpec((TM, tn), lambda i, j: (i, gate_blk0 + k * n_ct + j))

    def bias_spec(k):
        return pl.BlockSpec((1, tn), lambda i, j: (0, k * n_ct + j))

    def w_spec():
        return pl.BlockSpec((CONV_DIM, tn), lambda i, j: (0, j))

    return pl.pallas_call(
        _merge_kernel,
        out_shape=jax.ShapeDtypeStruct((n_rows, D), BF16),
        grid=(n_rows // TM, n_ct),
        in_specs=[y_spec(), y_spec(), y_spec(), gate_spec(0), gate_spec(1), gate_spec(2),
                  bias_spec(0), bias_spec(1), bias_spec(2), w_spec(), w_spec(), w_spec()],
        out_specs=pl.BlockSpec((TM, tn), lambda i, j: (i, j)),
        compiler_params=_params("parallel", "arbitrary"),
        name="merge_branches",
    )(ya, yb, yc, prest, prest, prest, b_gate, b_gate, b_gate, wa, wb, wc)


def _oproj_kernel(m_ref, w_ref, x_ref, g_ref, o_ref):
    o_ref[...] = x_ref[...] + g_ref[...] * _dot(m_ref[...], w_ref[...])


def _oproj_residual(m, w_o, xs, mod, gate_chunk, n_rows):
    tn = 512
    n_ct = D // tn
    return pl.pallas_call(
        _oproj_kernel,
        out_shape=jax.ShapeDtypeStruct((n_rows, D), F32),
        grid=(n_rows // TM, n_ct),
        in_specs=[
            pl.BlockSpec((TM, D), lambda i, j: (i, 0)),
            pl.BlockSpec((D, tn), lambda i, j: (0, j)),
            pl.BlockSpec((TM, tn), lambda i, j: (i, j)),
            pl.BlockSpec((None, 1, tn), lambda i, j: (i // (SEQ // TM), 0, gate_chunk * n_ct + j)),
        ],
        out_specs=pl.BlockSpec((TM, tn), lambda i, j: (i, j)),
        compiler_params=_params("parallel", "arbitrary"),
        name="oproj_residual",
    )(m, w_o, xs, mod)


def _first_lane_of(mask, lane):
    return jnp.min(jnp.where(mask, lane, float(ROUTE_PAD)), axis=-1, keepdims=True)


def _router_kernel(x_ref, g_ref, sc_ref, sh_ref, whi_ref, wlo_ref, b_ref, h_ref, route_ref, cnt_ref, cnt_scr):
    rows = 128

    @pl.when(pl.program_id(0) == 0)
    def _():
        cnt_scr[...] = jnp.zeros_like(cnt_scr)

    lane = lax.broadcasted_iota(jnp.int32, (rows, ROUTE_PAD), 1).astype(F32)
    tri = (lax.broadcasted_iota(jnp.int32, (rows, rows), 0)
           >= lax.broadcasted_iota(jnp.int32, (rows, rows), 1)).astype(BF16)
    neg = -jnp.inf

    def body(r, carry):
        sl = pl.ds(pl.multiple_of(r * rows, rows), rows)
        h = _norm_mod_rows(x_ref[sl, :], g_ref[...], sc_ref[...], sh_ref[...])
        h_ref[sl, :] = h
        h_hi = h.astype(BF16)
        h_lo = (h - h_hi.astype(F32)).astype(BF16)
        lg = _dot(h_hi, whi_ref[...]) + (_dot(h_hi, wlo_ref[...]) + _dot(h_lo, whi_ref[...])) + b_ref[...]

        is_group = lane < N_GROUPS
        g_logit = jnp.where(is_group, lg, neg)
        g_max = jnp.max(g_logit, axis=-1, keepdims=True)
        g_prob = 1.0 / jnp.sum(jnp.where(is_group, jnp.exp(g_logit - g_max), 0.0), axis=-1, keepdims=True)
        g_idx = _first_lane_of(g_logit == g_max, lane)

        lo = N_GROUPS + g_idx * EPG
        cand = jnp.where((lane >= lo) & (lane < lo + EPG), lg, neg)
        m0 = jnp.max(cand, axis=-1, keepdims=True)
        i0 = _first_lane_of(cand == m0, lane)
        cand = jnp.where(lane == i0, neg, cand)
        m1 = jnp.max(cand, axis=-1, keepdims=True)
        i1 = _first_lane_of(cand == m1, lane)
        t = jnp.exp(m1 - m0)
        w0 = g_prob / (1.0 + t)
        w1 = g_prob * t / (1.0 + t)

        hit0 = lane == i0
        hit1 = lane == i1
        hits = jnp.where(hit0 | hit1, 1.0, 0.0)
        prefix = _dot(tri, hits.astype(BF16))
        before = prefix - hits + cnt_scr[...]
        rank0 = jnp.sum(jnp.where(hit0, before, 0.0), axis=-1, keepdims=True)
        rank1 = jnp.sum(jnp.where(hit1, before, 0.0), axis=-1, keepdims=True)
        cnt_scr[...] = cnt_scr[...] + prefix[rows - 1:rows, :]

        fields = (i0 - N_GROUPS, i1 - N_GROUPS, w0, w1, rank0, rank1)
        route = jnp.zeros((rows, ROUTE_PAD), F32)
        for k, val in enumerate(fields):
            route = jnp.where(lane == k, val, route)
        route_ref[sl, :] = route
        return carry

    lax.fori_loop(0, TM // rows, body, 0)
    cnt_ref[...] = cnt_scr[...]


def _router(xs, gain, mod, sc_chunk, sh_chunk, w_hi, w_lo, bias, n_rows):
    return pl.pallas_call(
        _router_kernel,
        out_shape=[jax.ShapeDtypeStruct((n_rows, D), F32), jax.ShapeDtypeStruct((n_rows, ROUTE_PAD), F32),
                   jax.ShapeDtypeStruct((1, ROUTE_PAD), F32)],
        grid=(n_rows // TM,),
        in_specs=[
            pl.BlockSpec((TM, D), lambda i: (i, 0)),
            pl.BlockSpec((1, D), lambda i: (0, 0)),
            pl.BlockSpec((None, 1, D), lambda i: (i // (SEQ // TM), 0, sc_chunk)),
            pl.BlockSpec((None, 1, D), lambda i: (i // (SEQ // TM), 0, sh_chunk)),
            pl.BlockSpec((D, ROUTE_PAD), lambda i: (0, 0)),
            pl.BlockSpec((D, ROUTE_PAD), lambda i: (0, 0)),
            pl.BlockSpec((1, ROUTE_PAD), lambda i: (0, 0)),
        ],
        out_specs=[pl.BlockSpec((TM, D), lambda i: (i, 0)), pl.BlockSpec((TM, ROUTE_PAD), lambda i: (i, 0)),
                   pl.BlockSpec((1, ROUTE_PAD), lambda i: (0, 0))],
        scratch_shapes=[pltpu.VMEM((1, ROUTE_PAD), F32)],
        compiler_params=_params("arbitrary"),
        name="moe_router",
    )(xs, gain.reshape(1, D), mod, mod, w_hi, w_lo, bias)


def _row_copy(src_hbm, src_row, dst, dst_row, sem):
    return pltpu.make_async_copy(src_hbm.at[pl.ds(src_row, 1), :], dst.at[pl.ds(dst_row, 1), :], sem)


def _start_row_gather(src_hbm, dst, sem, n, index_of):
    def issue(r, carry):
        _row_copy(src_hbm, index_of(r), dst, r, sem).start()
        return carry

    lax.fori_loop(0, n, issue, 0, unroll=8)


def _wait_row_gather(src_hbm, dst, sem, n):
    def drain(r, carry):
        _row_copy(src_hbm, 0, dst, r, sem).wait()
        return carry

    lax.fori_loop(0, n, drain, 0, unroll=8)


EXPERT_K_SPLIT = 8


def _expert_kernel(be_ref, tok_ref, nused_ref, h_hbm, wgu_ref, wdn_ref, sw_ref, o_ref, xbuf0, xbuf1, sem):
    i = pl.program_id(0)
    n_used = nused_ref[0]
    bufs = (xbuf0, xbuf1)
    rows_per_group = MOE_BLOCK // EXPERT_K_SPLIT
    kc = D // EXPERT_K_SPLIT

    def issue(blk, s, r0, r1):
        for r in range(r0, r1):
            _row_copy(h_hbm, tok_ref[blk * MOE_BLOCK + r], bufs[s], r, sem.at[s]).start()

    @pl.when(i == 0)
    def _():
        issue(0, 0, 0, MOE_BLOCK)

    for s in range(2):
        mine = (i & 1) == s

        @pl.when(mine & (i <= n_used))
        def _():
            _wait_row_gather(h_hbm, bufs[s], sem.at[s], MOE_BLOCK)

        @pl.when(mine & (i < n_used))
        def _():
            x = bufs[s][...].astype(BF16)
            gu = None
            for kk in range(EXPERT_K_SPLIT):
                issue(i + 1, 1 - s, kk * rows_per_group, (kk + 1) * rows_per_group)
                part = _dot(x[:, kk * kc:(kk + 1) * kc], wgu_ref[kk * kc:(kk + 1) * kc, :].astype(BF16))
                gu = part if gu is None else gu + part
            gate = gu[:, :D_EXPERT]
            act = gate * jax.nn.sigmoid(gate) * gu[:, D_EXPERT:]
            o_ref[...] = _dot(act.astype(BF16), wdn_ref[...].astype(BF16)) * sw_ref[...]

    @pl.when(i >= n_used)
    def _():
        o_ref[...] = jnp.zeros_like(o_ref)


def _experts(block_expert, slot_tok, n_used, h2, w_gu, w_down, layer, slot_w, n_blocks):
    grid_spec = pltpu.PrefetchScalarGridSpec(
        num_scalar_prefetch=3,
        grid=(n_blocks,),
        in_specs=[
            pl.BlockSpec(memory_space=pl.ANY),
            pl.BlockSpec((None, None, D, 2 * D_EXPERT), lambda i, be, tok, nu: (layer, be[i], 0, 0)),
            pl.BlockSpec((None, None, D_EXPERT, D), lambda i, be, tok, nu: (layer, be[i], 0, 0)),
            pl.BlockSpec((MOE_BLOCK, 1), lambda i, be, tok, nu: (i, 0)),
        ],
        out_specs=pl.BlockSpec((MOE_BLOCK, D), lambda i, be, tok, nu: (i, 0)),
        scratch_shapes=[pltpu.VMEM((MOE_BLOCK, D), F32), pltpu.VMEM((MOE_BLOCK, D), F32),
                        pltpu.SemaphoreType.DMA((2,))],
    )
    return pl.pallas_call(
        _expert_kernel,
        out_shape=jax.ShapeDtypeStruct((n_blocks * MOE_BLOCK, D), F32),
        grid_spec=grid_spec,
        compiler_params=_params("arbitrary"),
        name="moe_experts",
    )(block_expert, slot_tok, n_used, h2, w_gu, w_down, slot_w)


def _combine_kernel(pos_ref, ys_hbm, x_ref, g_ref, o_ref, buf, sem):
    i = pl.program_id(0)
    n_tiles = pl.num_programs(0)
    rows = x_ref.shape[0]
    slot = i & 1

    def start(tile, s):
        _start_row_gather(ys_hbm, buf.at[s], sem.at[s], TOP_K * rows,
                          lambda r: pos_ref[(tile * rows + (r & (rows - 1))) * TOP_K + (r >> int(np.log2(rows)))])

    @pl.when(i == 0)
    def _():
        start(0, 0)

    _wait_row_gather(ys_hbm, buf.at[slot], sem.at[slot], TOP_K * rows)

    @pl.when(i + 1 < n_tiles)
    def _():
        start(i + 1, 1 - slot)

    o_ref[...] = x_ref[...] + g_ref[...] * (buf[slot, 0:rows, :] + buf[slot, rows:2 * rows, :])


def _combine(pos, ys, xs, mod, gate_chunk, n_rows):
    rows = 256
    grid_spec = pltpu.PrefetchScalarGridSpec(
        num_scalar_prefetch=1,
        grid=(n_rows // rows,),
        in_specs=[
            pl.BlockSpec(memory_space=pl.ANY),
            pl.BlockSpec((rows, D), lambda i, pos: (i, 0)),
            pl.BlockSpec((None, 1, D), lambda i, pos: (i // (SEQ // rows), 0, gate_chunk)),
        ],
        out_specs=pl.BlockSpec((rows, D), lambda i, pos: (i, 0)),
        scratch_shapes=[pltpu.VMEM((2, TOP_K * rows, D), F32), pltpu.SemaphoreType.DMA((2,))],
    )
    return pl.pallas_call(
        _combine_kernel,
        out_shape=jax.ShapeDtypeStruct((n_rows, D), F32),
        grid_spec=grid_spec,
        compiler_params=_params("arbitrary"),
        name="moe_combine",
    )(pos, ys, xs, mod)


def _slot_layout(route, counts, n_tok):
    n_assign = n_tok * TOP_K
    expert = route[:, 0:TOP_K].astype(jnp.int32)
    weight = route[:, TOP_K:2 * TOP_K]
    rank = route[:, 2 * TOP_K:3 * TOP_K].astype(jnp.int32)
    sizes = counts[0, N_GROUPS:N_GROUPS + N_EXPERTS].astype(jnp.int32)
    padded = (sizes + MOE_BLOCK - 1) // MOE_BLOCK * MOE_BLOCK
    pad_end = jnp.cumsum(padded)
    pad_start = pad_end - padded
    pos = (pad_start[expert] + rank).reshape(n_assign)
    n_blocks = (n_assign + MOE_BLOCK - 1) // MOE_BLOCK + N_EXPERTS + 1
    n_slots = n_blocks * MOE_BLOCK
    slot_assign = jnp.full((n_slots,), n_assign, jnp.int32).at[pos].set(
        jnp.arange(n_assign, dtype=jnp.int32), unique_indices=True)
    is_real = slot_assign < n_assign
    slot_tok = jnp.where(is_real, slot_assign // TOP_K, 0)
    weight_ext = jnp.concatenate([weight.reshape(n_assign), jnp.zeros((1,), F32)])
    slot_w = weight_ext[slot_assign]
    block_start = jnp.arange(n_blocks, dtype=jnp.int32) * MOE_BLOCK
    block_expert = jnp.minimum(jnp.sum((pad_end[None, :] <= block_start[:, None]).astype(jnp.int32), axis=1),
                               N_EXPERTS - 1)
    n_used = (pad_end[-1] // MOE_BLOCK).reshape(1)
    return block_expert, slot_tok, slot_w.reshape(n_slots, 1), pos, n_used, n_blocks


def _rope_tables():
    s = jnp.arange(SEQ)
    rows = (s // GRID_W).astype(F32)
    cols = (s % GRID_W).astype(F32)

    def angles(rot_dim):
        n = rot_dim // 4
        inv = ROPE_THETA ** (-jnp.arange(n, dtype=F32) / n)
        return jnp.concatenate([rows[:, None] * inv, cols[:, None] * inv], axis=-1)

    ang_a = angles(HD)
    cos_a, sin_a = jnp.cos(ang_a), jnp.sin(ang_a)
    tab_a = jnp.concatenate([cos_a, cos_a, -sin_a, sin_a], axis=-1)
    ident_a = jnp.concatenate([jnp.ones((T_CTX, HD), F32), jnp.zeros((T_CTX, HD), F32)], axis=-1)
    ang_c = angles(ROPE)
    cos_c, sin_c = jnp.cos(ang_c), jnp.sin(ang_c)
    half = ROPE // 2
    z_half = jnp.zeros((SEQ, half), F32)
    z_pad = jnp.zeros((SEQ, 128 - ROPE), F32)
    tab_c = jnp.concatenate([cos_c, cos_c, z_pad, -sin_c, z_half, z_pad, z_half, sin_c, z_pad], axis=-1)
    ident_c = jnp.concatenate([jnp.ones((T_CTX, ROPE), F32), jnp.zeros((T_CTX, 3 * 128 - ROPE), F32)], axis=-1)
    return jnp.concatenate([tab_a, ident_a], axis=0), jnp.concatenate([tab_c, ident_c], axis=0)


def _pad_head_cols(w, lead):
    w = w.reshape(lead, MLA_H, MLA_QK)
    return jnp.pad(w, ((0, 0), (0, 0), (0, MLA_PAD - MLA_QK))).reshape(lead, MLA_H * MLA_PAD)


def kernel(x, c, ctx, c_ctx, w_mod, b_mod, norm1, norm2, w_in, gqa_q_norm, gqa_k_norm, mla_q_lora_norm, w_uq,
           mla_kv_lora_norm, w_ukv, mla_q_norm, mla_k_norm, conv_w, w_out_a, w_out_b, w_out_c, b_gate, w_o,
           w_group, b_group, w_router, b_router, w_gu, w_down):
    xs = jnp.concatenate([x.reshape(T_LAT, D), ctx.reshape(T_CTX, D)], axis=0)
    cin = jnp.concatenate([c, c_ctx[None, :], jnp.zeros((3, D), F32)], axis=0)
    mod_all = _modulation(cin, w_mod, b_mod)
    rope_a, rope_c = _rope_tables()

    for l in range(DEPTH):
        last = l == DEPTH - 1
        n_rows = T_LAT if last else T_ALL
        mod = mod_all[l].reshape(8, 1, 6 * D)

        w_in_b = w_in[l].astype(BF16)
        w_kvq = jnp.concatenate(
            [w_in_b[:, :KV_COLS], jnp.zeros((D, 1024 - KV_COLS), BF16), w_in_b[:, KV_COLS:REST_OFF]], axis=1)
        w_rest = w_in_b[:, REST_OFF:]

        pkvq = _normed_matmul(xs, norm1[l], mod, 1, 0, w_kvq, T_ALL, KVQ_PAD // 2, "in_proj_kvq")
        prest = _normed_matmul(xs, norm1[l], mod, 1, 0, w_rest, n_rows, 1536, "in_proj_rest")

        ka, va, kc, vc, qa, qc = _attention_prep(
            pkvq, rope_a, rope_c,
            gqa_q_norm[l].reshape(1, HD), gqa_k_norm[l].reshape(1, HD),
            mla_q_lora_norm[l].reshape(1, Q_LORA), mla_kv_lora_norm[l].reshape(1, KV_LORA),
            jnp.pad(mla_q_norm[l], (0, MLA_PAD - MLA_QK)).reshape(1, MLA_PAD),
            jnp.pad(mla_k_norm[l], (0, MLA_PAD - MLA_QK)).reshape(1, MLA_PAD),
            _pad_head_cols(w_uq[l], Q_LORA).astype(BF16), w_ukv[l].astype(BF16))

        y_a = _attention(qa, ka, va, heads=KVH, group=GROUP, dk=HD, dv=HD, tq=ATTN_ROWS // GROUP, chunk=512,
                         with_ctx_queries=not last, name="gqa_attention")
        y_c = _attention(qc, kc, vc, heads=MLA_H, group=1, dk=MLA_PAD, dv=MLA_V, tq=ATTN_ROWS, chunk=512,
                         with_ctx_queries=not last, name="mla_attention")
        y_b = _conv_branch(prest, conv_w[l], n_rows)
        m = _merge(y_a, y_b, y_c, prest, b_gate[l].reshape(1, 3 * D),
                   w_out_a[l].astype(BF16), w_out_b[l].astype(BF16), w_out_c[l].astype(BF16), n_rows)
        xs = _oproj_residual(m, w_o[l].astype(BF16), xs, mod, 2, n_rows)

        w_r = jnp.concatenate([w_group[l], w_router[l], jnp.zeros((D, ROUTE_PAD - N_GROUPS - N_EXPERTS), F32)], axis=1)
        w_r_hi = w_r.astype(BF16)
        w_r_lo = (w_r - w_r_hi.astype(F32)).astype(BF16)
        b_r = jnp.concatenate([b_group[l], b_router[l], jnp.zeros((ROUTE_PAD - N_GROUPS - N_EXPERTS,), F32)])
        h2, route, counts = _router(xs, norm2[l], mod, 4, 3, w_r_hi, w_r_lo, b_r.reshape(1, ROUTE_PAD), n_rows)
        block_expert, slot_tok, slot_w, pos, n_used, n_blocks = _slot_layout(route, counts, n_rows)
        ys = _experts(block_expert, slot_tok, n_used, h2, w_gu, w_down, l, slot_w, n_blocks)
        xs = _combine(pos, ys, xs, mod, 5, n_rows)

    return xs[:T_LAT].reshape(BATCH, SEQ, D)
```

```python
import functools

import numpy as np
import jax
import jax.numpy as jnp
from jax import lax
from jax.experimental import pallas as pl
from jax.experimental.pallas import tpu as pltpu

D = 2048
BATCH = 4
SEQ = 4096
DEPTH = 2
GRID_W = 64
CTX = 256
ROPE_THETA = 10000.0
EPS = 1e-6
KVH = 2
GROUP = 4
HD = 128
CONV_DIM = 1024
MLA_H = 8
Q_LORA = 512
KV_LORA = 256
NOPE = 128
ROPE = 64
MLA_V = 128
MLA_QK = NOPE + ROPE
MLA_PAD = 256
N_GROUPS = 4
EPG = 8
N_EXPERTS = N_GROUPS * EPG
TOP_K = 2
D_EXPERT = 512
MOE_BLOCK = 256

T_LAT = BATCH * SEQ
T_CTX = BATCH * CTX
T_ALL = T_LAT + T_CTX
KV_COLS = 2 * KVH * HD + KV_LORA + ROPE
Q_COLS = KVH * GROUP * HD + Q_LORA
REST_OFF = KV_COLS + Q_COLS
KVQ_PAD = 1024 + Q_COLS
REST_COLS = 3 * CONV_DIM + 3 * D
ROUTE_PAD = 128

V7X_VMEM_LIMIT = 56 * 1024 * 1024
TM = 1024
ATTN_ROWS = 1024
BF16_SUBLANES = 16
LOG2_E = float(np.log2(np.e))

F32 = jnp.float32
BF16 = jnp.bfloat16


def _params(*sem):
    return pltpu.CompilerParams(dimension_semantics=sem, vmem_limit_bytes=V7X_VMEM_LIMIT)


def _dot(a, b):
    return jnp.dot(a, b, preferred_element_type=F32)


def _mod_kernel(c_ref, w_ref, b_ref, o_ref):
    c = c_ref[...]
    a = (c * jax.nn.sigmoid(c)).astype(BF16)
    o_ref[...] = _dot(a, w_ref[...].astype(BF16)) + b_ref[...]


def _modulation(cin, w_mod, b_mod):
    tn = 1024
    return pl.pallas_call(
        _mod_kernel,
        out_shape=jax.ShapeDtypeStruct((DEPTH, 8, 6 * D), F32),
        grid=(DEPTH, 6 * D // tn),
        in_specs=[
            pl.BlockSpec((8, D), lambda l, j: (0, 0)),
            pl.BlockSpec((None, D, tn), lambda l, j: (l, 0, j)),
            pl.BlockSpec((None, 1, tn), lambda l, j: (l, 0, j)),
        ],
        out_specs=pl.BlockSpec((None, 8, tn), lambda l, j: (l, 0, j)),
        compiler_params=_params("parallel", "parallel"),
        name="modulation",
    )(cin, w_mod, b_mod.reshape(DEPTH, 1, 6 * D))


def _norm_mod_rows(x, g, sc, sh):
    ms = jnp.mean(x * x, axis=-1, keepdims=True)
    return (x * lax.rsqrt(ms + EPS) * g) * (1.0 + sc) + sh


def _normed_matmul_kernel(x_ref, g_ref, sc_ref, sh_ref, w_ref, o_ref, h_scr):
    rows = 128

    @pl.when(pl.program_id(1) == 0)
    def _():
        def body(r, carry):
            sl = pl.ds(pl.multiple_of(r * rows, rows), rows)
            h_scr[sl, :] = _norm_mod_rows(x_ref[sl, :], g_ref[...], sc_ref[...], sh_ref[...]).astype(BF16)
            return carry

        lax.fori_loop(0, TM // rows, body, 0)

    o_ref[...] = _dot(h_scr[...], w_ref[...]).astype(o_ref.dtype)


def _normed_matmul(xs, gain, mod, sc_chunk, sh_chunk, w, n_rows, tn, name):
    n = w.shape[1]
    return pl.pallas_call(
        _normed_matmul_kernel,
        out_shape=jax.ShapeDtypeStruct((n_rows, n), BF16),
        grid=(n_rows // TM, n // tn),
        in_specs=[
            pl.BlockSpec((TM, D), lambda i, j: (i, 0)),
            pl.BlockSpec((1, D), lambda i, j: (0, 0)),
            pl.BlockSpec((None, 1, D), lambda i, j: (i // (SEQ // TM), 0, sc_chunk)),
            pl.BlockSpec((None, 1, D), lambda i, j: (i // (SEQ // TM), 0, sh_chunk)),
            pl.BlockSpec((D, tn), lambda i, j: (0, j)),
        ],
        out_specs=pl.BlockSpec((TM, tn), lambda i, j: (i, j)),
        scratch_shapes=[pltpu.VMEM((TM, D), BF16)],
        compiler_params=_params("parallel", "arbitrary"),
        name=name,
    )(xs, gain.reshape(1, D), mod, mod, w)


def _rms_lanes(x, width):
    return lax.rsqrt(jnp.sum(x * x, axis=-1, keepdims=True) * (1.0 / width) + EPS)


def _prep_kernel(p_ref, ra_ref, rc_ref, gq_ref, gk_ref, gql_ref, gkl_ref, gmq_ref, gmk_ref, wuq_ref, wukv_ref,
                 ka_ref, va_ref, kc_ref, vc_ref, qa_ref, qc_ref):
    cos_a = ra_ref[:, 0:HD]
    sin_a = ra_ref[:, HD:2 * HD]
    cos_c = rc_ref[:, 0:128]
    sin_up = rc_ref[:, 128:256]
    sin_dn = rc_ref[:, 256:384]

    def rope_a(x):
        return x * cos_a + pltpu.roll(x, HD // 2, 1) * sin_a

    def rope_c(x):
        return x * cos_c + pltpu.roll(x, 128 - ROPE // 2, 1) * sin_up + pltpu.roll(x, ROPE // 2, 1) * sin_dn

    for h in range(KVH):
        k = p_ref[:, h * HD:(h + 1) * HD].astype(F32)
        ka_ref[:, h * HD:(h + 1) * HD] = rope_a(k * _rms_lanes(k, HD) * gk_ref[...]).astype(BF16)
    ones = jnp.ones((p_ref.shape[0], 128), BF16)
    for h in range(KVH):
        va_ref[:, 2 * h * HD:(2 * h + 1) * HD] = p_ref[:, (KVH + h) * HD:(KVH + h + 1) * HD]
        va_ref[:, (2 * h + 1) * HD:(2 * h + 2) * HD] = ones

    ckv = p_ref[:, 512:768].astype(F32)
    ckv_n = (ckv * _rms_lanes(ckv, KV_LORA) * gkl_ref[...]).astype(BF16)
    kv = _dot(ckv_n, wukv_ref[...])
    krope = p_ref[:, 768:896].astype(F32)
    krope_ss = jnp.sum(krope * krope, axis=-1, keepdims=True)
    krope_rot = rope_c(krope * gmk_ref[:, 128:256])
    for h in range(MLA_H):
        kn = kv[:, h * 256:h * 256 + NOPE]
        r = lax.rsqrt((jnp.sum(kn * kn, axis=-1, keepdims=True) + krope_ss) * (1.0 / MLA_QK) + EPS)
        kc_ref[:, h * MLA_PAD:h * MLA_PAD + NOPE] = (kn * r * gmk_ref[:, 0:128]).astype(BF16)
        kc_ref[:, h * MLA_PAD + NOPE:(h + 1) * MLA_PAD] = (krope_rot * r).astype(BF16)
        vc_ref[:, 2 * h * MLA_V:(2 * h + 1) * MLA_V] = kv[:, h * 256 + NOPE:(h + 1) * 256].astype(BF16)
        vc_ref[:, (2 * h + 1) * MLA_V:(2 * h + 2) * MLA_V] = ones

    for h in range(KVH * GROUP):
        q = p_ref[:, 1024 + h * HD:1024 + (h + 1) * HD].astype(F32)
        q = rope_a(q * _rms_lanes(q, HD) * gq_ref[...]) * (HD ** -0.5 * LOG2_E)
        qa_ref[:, h * HD:(h + 1) * HD] = q.astype(BF16)

    cq = p_ref[:, 2048:2048 + Q_LORA].astype(F32)
    cq_n = (cq * _rms_lanes(cq, Q_LORA) * gql_ref[...]).astype(BF16)
    qc = _dot(cq_n, wuq_ref[...])
    for h in range(MLA_H):
        qn = qc[:, h * MLA_PAD:h * MLA_PAD + NOPE]
        qr = qc[:, h * MLA_PAD + NOPE:(h + 1) * MLA_PAD]
        r = lax.rsqrt((jnp.sum(qn * qn, axis=-1, keepdims=True) + jnp.sum(qr * qr, axis=-1, keepdims=True))
                      * (1.0 / MLA_QK) + EPS) * (MLA_QK ** -0.5 * LOG2_E)
        qc_ref[:, h * MLA_PAD:h * MLA_PAD + NOPE] = (qn * r * gmq_ref[:, 0:128]).astype(BF16)
        qc_ref[:, h * MLA_PAD + NOPE:(h + 1) * MLA_PAD] = (rope_c(qr * gmq_ref[:, 128:256]) * r).astype(BF16)


def _attention_prep(pkvq, rope_a, rope_c, gq, gk, gql, gkl, gmq, gmk, wuq, wukv):
    tm = 256
    n_lat = T_LAT // tm
    per_seq = SEQ // tm

    def rope_idx(i):
        return jnp.where(i < n_lat, i % per_seq, per_seq + i - n_lat)

    def full(shape):
        return pl.BlockSpec(shape, lambda i: (0, 0))

    def rows(width):
        return pl.BlockSpec((tm, width), lambda i: (i, 0))

    out_widths = (KVH * HD, 2 * KVH * HD, MLA_H * MLA_PAD, 2 * MLA_H * MLA_V, KVH * GROUP * HD, MLA_H * MLA_PAD)
    return pl.pallas_call(
        _prep_kernel,
        out_shape=[jax.ShapeDtypeStruct((T_ALL, w), BF16) for w in out_widths],
        grid=(T_ALL // tm,),
        in_specs=[
            rows(KVQ_PAD),
            pl.BlockSpec((tm, 2 * HD), lambda i: (rope_idx(i), 0)),
            pl.BlockSpec((tm, 3 * 128), lambda i: (rope_idx(i), 0)),
            full((1, HD)), full((1, HD)), full((1, Q_LORA)), full((1, KV_LORA)),
            full((1, MLA_PAD)), full((1, MLA_PAD)),
            full((Q_LORA, MLA_H * MLA_PAD)), full((KV_LORA, MLA_H * (NOPE + MLA_V))),
        ],
        out_specs=[rows(w) for w in out_widths],
        compiler_params=_params("parallel"),
        name="attention_prep",
    )(pkvq, rope_a, rope_c, gq, gk, gql, gkl, gmq, gmk, wuq, wukv)


def _attn_kernel(q_ref, *refs, group, dk, dv, chunk):
    segments = [(refs[2 * i], refs[2 * i + 1]) for i in range((len(refs) - 1) // 2)]
    o_ref = refs[-1]
    tq = q_ref.shape[0]
    if group > 1:
        q = jnp.concatenate([q_ref[:, g * dk:(g + 1) * dk] for g in range(group)], axis=0)
    else:
        q = q_ref[...]
    m = None
    for k_ref, v_ref in segments:
        n_keys = k_ref.shape[0]
        step = min(chunk, n_keys)
        for c0 in range(0, n_keys, step):
            s = lax.dot_general(q, k_ref[c0:c0 + step, :], (((1,), (1,)), ((), ())), preferred_element_type=F32)
            c_max = jnp.max(s, axis=-1, keepdims=True)
            if m is None:
                m = c_max
                acc = _dot(jnp.exp2(s - m).astype(BF16), v_ref[c0:c0 + step, :])
            else:
                m_new = jnp.maximum(m, c_max)
                acc = jnp.exp2(m - m_new) * acc + _dot(jnp.exp2(s - m_new).astype(BF16), v_ref[c0:c0 + step, :])
                m = m_new
    o = acc[:, :dv] / acc[:, dv:]
    for g in range(group):
        o_ref[:, g * dv:(g + 1) * dv] = o[g * tq:(g + 1) * tq].astype(o_ref.dtype)


def _attention_latent(q, k, v, *, heads, group, dk, dv, tq, chunk, name):
    n_qt = SEQ // tq
    kern = functools.partial(_attn_kernel, group=group, dk=dk, dv=dv, chunk=chunk)
    return pl.pallas_call(
        kern,
        out_shape=jax.ShapeDtypeStruct((T_LAT, heads * group * dv), BF16),
        grid=(BATCH, heads, n_qt),
        in_specs=[
            pl.BlockSpec((tq, group * dk), lambda b, h, qi: (b * n_qt + qi, h)),
            pl.BlockSpec((CTX, dk), lambda b, h, qi: (T_LAT // CTX + b, h)),
            pl.BlockSpec((CTX, 2 * dv), lambda b, h, qi: (T_LAT // CTX + b, h)),
            pl.BlockSpec((SEQ, dk), lambda b, h, qi: (b, h)),
            pl.BlockSpec((SEQ, 2 * dv), lambda b, h, qi: (b, h)),
        ],
        out_specs=pl.BlockSpec((tq, group * dv), lambda b, h, qi: (b * n_qt + qi, h)),
        compiler_params=_params("parallel", "parallel", "arbitrary"),
        name=name,
    )(q, k, v, k, v)


def _attention_context(q, k, v, *, heads, group, dk, dv, name):
    kern = functools.partial(_attn_kernel, group=group, dk=dk, dv=dv, chunk=CTX)
    ctx0 = T_LAT // CTX
    return pl.pallas_call(
        kern,
        out_shape=jax.ShapeDtypeStruct((T_CTX, heads * group * dv), BF16),
        grid=(BATCH, heads),
        in_specs=[
            pl.BlockSpec((CTX, group * dk), lambda b, h: (ctx0 + b, h)),
            pl.BlockSpec((CTX, dk), lambda b, h: (ctx0 + b, h)),
            pl.BlockSpec((CTX, 2 * dv), lambda b, h: (ctx0 + b, h)),
        ],
        out_specs=pl.BlockSpec((CTX, group * dv), lambda b, h: (b, h)),
        compiler_params=_params("parallel", "parallel"),
        name=name,
    )(q, k, v)


def _attention(q, k, v, *, heads, group, dk, dv, tq, chunk, with_ctx_queries, name):
    y = _attention_latent(q, k, v, heads=heads, group=group, dk=dk, dv=dv, tq=tq, chunk=chunk, name=name)
    if with_ctx_queries:
        y_ctx = _attention_context(q, k, v, heads=heads, group=group, dk=dk, dv=dv, name=name + "_ctx")
        y = jnp.concatenate([y, y_ctx], axis=0)
    return y


def _conv_kernel(xb_ref, gb_ref, gc_ref, xbp_ref, gcp_ref, xbn_ref, gcn_ref, w_ref, o_ref):
    i = pl.program_id(0)
    z = gc_ref[...].astype(F32) * xb_ref[...].astype(F32)
    last = BF16_SUBLANES - 1
    z_prev = gcp_ref[last:last + 1, :].astype(F32) * xbp_ref[last:last + 1, :].astype(F32)
    z_next = gcn_ref[0:1, :].astype(F32) * xbn_ref[0:1, :].astype(F32)
    row = lax.broadcasted_iota(jnp.int32, (TM, 1), 0)
    seq_len = jnp.where(i < T_LAT // TM, SEQ, CTX)
    pos = (i * TM + row) & (seq_len - 1)
    z_dn = jnp.where(row == 0, z_prev, pltpu.roll(z, 1, 0))
    z_up = jnp.where(row == TM - 1, z_next, pltpu.roll(z, TM - 1, 0))
    z_dn = jnp.where(pos == 0, 0.0, z_dn)
    z_up = jnp.where(pos == seq_len - 1, 0.0, z_up)
    y = gb_ref[...].astype(F32) * (w_ref[0:1, :] * z_dn + w_ref[1:2, :] * z + w_ref[2:3, :] * z_up)
    o_ref[...] = y.astype(BF16)


def _conv_branch(prest, conv_w, n_rows):
    tc = 256
    n_ct = CONV_DIM // tc
    halo = TM // BF16_SUBLANES
    last_halo = n_rows // BF16_SUBLANES - 1

    def main(col0):
        return pl.BlockSpec((TM, tc), lambda i, j: (i, col0 * n_ct + j))

    def prev(col0):
        return pl.BlockSpec((BF16_SUBLANES, tc), lambda i, j: (jnp.maximum(i * halo - 1, 0), col0 * n_ct + j))

    def nxt(col0):
        return pl.BlockSpec((BF16_SUBLANES, tc), lambda i, j: (jnp.minimum((i + 1) * halo, last_halo), col0 * n_ct + j))

    return pl.pallas_call(
        _conv_kernel,
        out_shape=jax.ShapeDtypeStruct((n_rows, CONV_DIM), BF16),
        grid=(n_rows // TM, n_ct),
        in_specs=[main(0), main(1), main(2), prev(0), prev(2), nxt(0), nxt(2),
                  pl.BlockSpec((3, tc), lambda i, j: (0, j))],
        out_specs=pl.BlockSpec((TM, tc), lambda i, j: (i, j)),
        compiler_params=_params("parallel", "parallel"),
        name="conv_branch",
    )(prest, prest, prest, prest, prest, prest, prest, conv_w)


def _merge_kernel(ya_ref, yb_ref, yc_ref, ga_ref, gb_ref, gc_ref, ba_ref, bb_ref, bc_ref,
                  wa_ref, wb_ref, wc_ref, o_ref):
    def branch(y_ref, w_ref, g_ref, b_ref):
        return jax.nn.sigmoid(g_ref[...].astype(F32) + b_ref[...]) * _dot(y_ref[...], w_ref[...])

    m = branch(ya_ref, wa_ref, ga_ref, ba_ref)
    m = m + branch(yb_ref, wb_ref, gb_ref, bb_ref)
    m = m + branch(yc_ref, wc_ref, gc_ref, bc_ref)
    o_ref[...] = m.astype(BF16)


def _merge(ya, yb, yc, prest, b_gate, wa, wb, wc, n_rows):
    tn = 512
    n_ct = D // tn
    gate_blk0 = 3 * CONV_DIM // tn

    def y_spec():
        return pl.BlockSpec((TM, CONV_DIM), lambda i, j: (i, 0))

    def gate_spec(k):
        return pl.BlockSpec((TM, tn), lambda i, j: (i, gate_blk0 + k * n_ct + j))

    def bias_spec(k):
        return pl.BlockSpec((1, tn), lambda i, j: (0, k * n_ct + j))

    def w_spec():
        return pl.BlockSpec((CONV_DIM, tn), lambda i, j: (0, j))

    return pl.pallas_call(
        _merge_kernel,
        out_shape=jax.ShapeDtypeStruct((n_rows, D), BF16),
        grid=(n_rows // TM, n_ct),
        in_specs=[y_spec(), y_spec(), y_spec(), gate_spec(0), gate_spec(1), gate_spec(2),
                  bias_spec(0), bias_spec(1), bias_spec(2), w_spec(), w_spec(), w_spec()],
        out_specs=pl.BlockSpec((TM, tn), lambda i, j: (i, j)),
        compiler_params=_params("parallel", "arbitrary"),
        name="merge_branches",
    )(ya, yb, yc, prest, prest, prest, b_gate, b_gate, b_gate, wa, wb, wc)


def _oproj_kernel(m_ref, w_ref, x_ref, g_ref, o_ref):
    o_ref[...] = x_ref[...] + g_ref[...] * _dot(m_ref[...], w_ref[...])


def _oproj_residual(m, w_o, xs, mod, gate_chunk, n_rows):
    tn = 512
    n_ct = D // tn
    return pl.pallas_call(
        _oproj_kernel,
        out_shape=jax.ShapeDtypeStruct((n_rows, D), F32),
        grid=(n_rows // TM, n_ct),
        in_specs=[
            pl.BlockSpec((TM, D), lambda i, j: (i, 0)),
            pl.BlockSpec((D, tn), lambda i, j: (0, j)),
            pl.BlockSpec((TM, tn), lambda i, j: (i, j)),
            pl.BlockSpec((None, 1, tn), lambda i, j: (i // (SEQ // TM), 0, gate_chunk * n_ct + j)),
        ],
        out_specs=pl.BlockSpec((TM, tn), lambda i, j: (i, j)),
        compiler_params=_params("parallel", "arbitrary"),
        name="oproj_residual",
    )(m, w_o, xs, mod)


def _router_kernel(x_ref, g_ref, sc_ref, sh_ref, whi_ref, wlo_ref, b_ref, h_ref, route_ref, cnt_ref, cnt_scr):
    cols = 128

    @pl.when(pl.program_id(0) == 0)
    def _():
        cnt_scr[...] = jnp.zeros_like(cnt_scr)

    tri = (lax.broadcasted_iota(jnp.int32, (cols, cols), 0)
           <= lax.broadcasted_iota(jnp.int32, (cols, cols), 1)).astype(BF16)
    sub = lax.broadcasted_iota(jnp.int32, (EPG, cols), 0).astype(F32)
    e_row = lax.broadcasted_iota(jnp.int32, (N_EXPERTS, cols), 0).astype(F32)
    nt_dims = (((1,), (1,)), ((), ()))
    neg = -jnp.inf

    def body(r, carry):
        sl = pl.ds(pl.multiple_of(r * cols, cols), cols)
        h = _norm_mod_rows(x_ref[sl, :], g_ref[...], sc_ref[...], sh_ref[...])
        for c in range(D // 128):
            h_ref[sl, c, :] = h[:, c * 128:(c + 1) * 128]
        h_hi = h.astype(BF16)
        h_lo = (h - h_hi.astype(F32)).astype(BF16)

        def wt_dot(w_ref, act):
            return lax.dot_general(w_ref[...], act, nt_dims, preferred_element_type=F32)

        lg = wt_dot(whi_ref, h_hi) + (wt_dot(wlo_ref, h_hi) + wt_dot(whi_ref, h_lo)) + b_ref[...]

        g_logit = [lg[N_EXPERTS + g:N_EXPERTS + g + 1, :] for g in range(N_GROUPS)]
        g_max = functools.reduce(jnp.maximum, g_logit)
        g_prob = 1.0 / functools.reduce(lambda a, b: a + b, [jnp.exp(v - g_max) for v in g_logit])
        g_idx = jnp.full_like(g_max, float(N_GROUPS - 1))
        for g in range(N_GROUPS - 2, -1, -1):
            g_idx = jnp.where(g_logit[g] == g_max, float(g), g_idx)

        cand = lg[(N_GROUPS - 1) * EPG:N_GROUPS * EPG, :]
        for g in range(N_GROUPS - 2, -1, -1):
            cand = jnp.where(g_idx == float(g), lg[g * EPG:(g + 1) * EPG, :], cand)
        m0 = jnp.max(cand, axis=0, keepdims=True)
        i0 = jnp.min(jnp.where(cand == m0, sub, float(EPG)), axis=0, keepdims=True)
        cand = jnp.where(sub == i0, neg, cand)
        m1 = jnp.max(cand, axis=0, keepdims=True)
        i1 = jnp.min(jnp.where(cand == m1, sub, float(EPG)), axis=0, keepdims=True)
        t = jnp.exp(m1 - m0)
        w0 = g_prob / (1.0 + t)
        w1 = g_prob * t / (1.0 + t)
        e0 = g_idx * EPG + i0
        e1 = g_idx * EPG + i1

        hit0 = e_row == e0
        hit1 = e_row == e1
        hits = jnp.where(hit0 | hit1, 1.0, 0.0)
        prefix = _dot(hits.astype(BF16), tri)
        before = prefix - hits + cnt_scr[...]
        rank0 = jnp.sum(jnp.where(hit0, before, 0.0), axis=0, keepdims=True)
        rank1 = jnp.sum(jnp.where(hit1, before, 0.0), axis=0, keepdims=True)
        cnt_scr[...] = cnt_scr[...] + prefix[:, cols - 1:cols]
        route_ref[:, sl] = jnp.concatenate([e0, e1, w0, w1, rank0, rank1, jnp.zeros((2, cols), F32)], axis=0)
        return carry

    lax.fori_loop(0, TM // cols, body, 0, unroll=2)
    cnt_ref[...] = cnt_scr[...]


def _router(xs, gain, mod, sc_chunk, sh_chunk, w_hi, w_lo, bias, n_rows):
    return pl.pallas_call(
        _router_kernel,
        out_shape=[jax.ShapeDtypeStruct((n_rows, D // 128, 128), F32), jax.ShapeDtypeStruct((8, n_rows), F32),
                   jax.ShapeDtypeStruct((N_EXPERTS, 1), F32)],
        grid=(n_rows // TM,),
        in_specs=[
            pl.BlockSpec((TM, D), lambda i: (i, 0)),
            pl.BlockSpec((1, D), lambda i: (0, 0)),
            pl.BlockSpec((None, 1, D), lambda i: (i // (SEQ // TM), 0, sc_chunk)),
            pl.BlockSpec((None, 1, D), lambda i: (i // (SEQ // TM), 0, sh_chunk)),
            pl.BlockSpec((ROUTE_PAD, D), lambda i: (0, 0)),
            pl.BlockSpec((ROUTE_PAD, D), lambda i: (0, 0)),
            pl.BlockSpec((ROUTE_PAD, 1), lambda i: (0, 0)),
        ],
        out_specs=[pl.BlockSpec((TM, D // 128, 128), lambda i: (i, 0, 0)), pl.BlockSpec((8, TM), lambda i: (0, i)),
                   pl.BlockSpec((N_EXPERTS, 1), lambda i: (0, 0))],
        scratch_shapes=[pltpu.VMEM((N_EXPERTS, 1), F32)],
        compiler_params=_params("arbitrary"),
        name="moe_router",
    )(xs, gain.reshape(1, D), mod, mod, w_hi, w_lo, bias)


def _slab_copy(src_hbm, token, dst, dst_row, sem):
    return pltpu.make_async_copy(src_hbm.at[token], dst.at[:, dst_row, :], sem)


def _start_slab_gather(src_hbm, dst, sem, n, index_of):
    def issue(r, carry):
        _slab_copy(src_hbm, index_of(r), dst, r, sem).start()
        return carry

    lax.fori_loop(0, n, issue, 0, unroll=8)


def _wait_slab_gather(src_hbm, dst, sem, n):
    def drain(r, carry):
        _slab_copy(src_hbm, 0, dst, r, sem).wait()
        return carry

    lax.fori_loop(0, n, drain, 0, unroll=8)


EXPERT_K_SPLIT = 8
N_CHUNKS = D // 128


def _expert_kernel(be_ref, tok_ref, nused_ref, h_hbm, wgu_ref, wdn_ref, sw_ref, ys_hbm,
                   xbuf0, xbuf1, ybuf0, ybuf1, zbuf, sem_in, sem_out):
    i = pl.program_id(0)
    n_used = nused_ref[0]
    xbufs = (xbuf0, xbuf1)
    ybufs = (ybuf0, ybuf1)
    rows_per_group = MOE_BLOCK // EXPERT_K_SPLIT
    chunks_per_group = N_CHUNKS // EXPERT_K_SPLIT
    kc = D // EXPERT_K_SPLIT

    def issue(blk, s, r0, r1):
        for r in range(r0, r1):
            _slab_copy(h_hbm, tok_ref[blk * MOE_BLOCK + r], xbufs[s], r, sem_in.at[s]).start()

    def out_copy(blk, s, c, src=None):
        src = ybufs[s] if src is None else src
        return pltpu.make_async_copy(src.at[c], ys_hbm.at[pl.ds(blk * MOE_BLOCK, MOE_BLOCK), c, :], sem_out.at[s])

    @pl.when(i == 0)
    def _():
        issue(0, 0, 0, MOE_BLOCK)
        zbuf[...] = jnp.zeros_like(zbuf)

    for s in range(2):
        mine = (i & 1) == s

        @pl.when(mine & (i <= n_used))
        def _():
            _wait_slab_gather(h_hbm, xbufs[s], sem_in.at[s], MOE_BLOCK)

        @pl.when(mine & (i >= 2) & (i < n_used + 2))
        def _():
            for c in range(N_CHUNKS):
                out_copy(0, s, c).wait()

        @pl.when(mine & (i < n_used))
        def _():
            gu = None
            for kk in range(EXPERT_K_SPLIT):
                issue(i + 1, 1 - s, kk * rows_per_group, (kk + 1) * rows_per_group)
                xk = jnp.concatenate([xbufs[s][kk * chunks_per_group + c] for c in range(chunks_per_group)], axis=1)
                part = _dot(xk.astype(BF16), wgu_ref[kk * kc:(kk + 1) * kc, :].astype(BF16))
                gu = part if gu is None else gu + part
            gate = gu[:, :D_EXPERT]
            act = gate * jax.nn.sigmoid(gate) * gu[:, D_EXPERT:]
            y = _dot(act.astype(BF16), wdn_ref[...].astype(BF16)) * sw_ref[...]
            for c in range(N_CHUNKS):
                ybufs[s][c] = y[:, c * 128:(c + 1) * 128]
            for c in range(N_CHUNKS):
                out_copy(i, s, c).start()

        @pl.when(mine & (i >= n_used))
        def _():
            for c in range(N_CHUNKS):
                out_copy(i, s, c, zbuf).start()
            for c in range(N_CHUNKS):
                out_copy(i, s, c, zbuf).wait()


def _experts(block_expert, slot_tok, n_used, h2, w_gu, w_down, layer, slot_w, n_blocks):
    grid_spec = pltpu.PrefetchScalarGridSpec(
        num_scalar_prefetch=3,
        grid=(n_blocks,),
        in_specs=[
            pl.BlockSpec(memory_space=pl.ANY),
            pl.BlockSpec((None, None, D, 2 * D_EXPERT), lambda i, be, tok, nu: (layer, be[i], 0, 0)),
            pl.BlockSpec((None, None, D_EXPERT, D), lambda i, be, tok, nu: (layer, be[i], 0, 0)),
            pl.BlockSpec((MOE_BLOCK, 1), lambda i, be, tok, nu: (i, 0)),
        ],
        out_specs=pl.BlockSpec(memory_space=pl.ANY),
        scratch_shapes=[pltpu.VMEM((N_CHUNKS, MOE_BLOCK, 128), F32) for _ in range(5)]
        + [pltpu.SemaphoreType.DMA((2,)), pltpu.SemaphoreType.DMA((2,))],
    )
    return pl.pallas_call(
        _expert_kernel,
        out_shape=jax.ShapeDtypeStruct((n_blocks * MOE_BLOCK, N_CHUNKS, 128), F32),
        grid_spec=grid_spec,
        compiler_params=_params("arbitrary"),
        name="moe_experts",
    )(block_expert, slot_tok, n_used, h2, w_gu, w_down, slot_w)


def _combine_kernel(pos_ref, ys_hbm, x_ref, g_ref, o_ref, buf, sem):
    i = pl.program_id(0)
    n_tiles = pl.num_programs(0)
    rows = x_ref.shape[0]
    slot = i & 1

    def start(tile, s):
        _start_slab_gather(ys_hbm, buf.at[s], sem.at[s], TOP_K * rows,
                           lambda r: pos_ref[(tile * rows + (r & (rows - 1))) * TOP_K + (r >> int(np.log2(rows)))])

    @pl.when(i == 0)
    def _():
        start(0, 0)

    _wait_slab_gather(ys_hbm, buf.at[slot], sem.at[slot], TOP_K * rows)

    @pl.when(i + 1 < n_tiles)
    def _():
        start(i + 1, 1 - slot)

    for c in range(N_CHUNKS):
        lanes = slice(c * 128, (c + 1) * 128)
        o_ref[:, lanes] = x_ref[:, lanes] + g_ref[:, lanes] * (buf[slot, c, 0:rows, :] + buf[slot, c, rows:2 * rows, :])


def _combine(pos, ys, xs, mod, gate_chunk, n_rows):
    rows = 256
    grid_spec = pltpu.PrefetchScalarGridSpec(
        num_scalar_prefetch=1,
        grid=(n_rows // rows,),
        in_specs=[
            pl.BlockSpec(memory_space=pl.ANY),
            pl.BlockSpec((rows, D), lambda i, pos: (i, 0)),
            pl.BlockSpec((None, 1, D), lambda i, pos: (i // (SEQ // rows), 0, gate_chunk)),
        ],
        out_specs=pl.BlockSpec((rows, D), lambda i, pos: (i, 0)),
        scratch_shapes=[pltpu.VMEM((2, N_CHUNKS, TOP_K * rows, 128), F32), pltpu.SemaphoreType.DMA((2,))],
    )
    return pl.pallas_call(
        _combine_kernel,
        out_shape=jax.ShapeDtypeStruct((n_rows, D), F32),
        grid_spec=grid_spec,
        compiler_params=_params("arbitrary"),
        name="moe_combine",
    )(pos, ys, xs, mod)


def _slot_layout(route, counts, n_tok):
    n_assign = n_tok * TOP_K
    expert = route[0:TOP_K, :].T.astype(jnp.int32)
    weight = route[TOP_K:2 * TOP_K, :].T
    rank = route[2 * TOP_K:3 * TOP_K, :].T.astype(jnp.int32)
    sizes = counts[:, 0].astype(jnp.int32)
    padded = (sizes + MOE_BLOCK - 1) // MOE_BLOCK * MOE_BLOCK
    pad_end = jnp.cumsum(padded)
    pad_start = pad_end - padded
    pos = (pad_start[expert] + rank).reshape(n_assign)
    n_blocks = (n_assign + MOE_BLOCK - 1) // MOE_BLOCK + N_EXPERTS + 2
    n_slots = n_blocks * MOE_BLOCK
    slot_assign = jnp.full((n_slots,), n_assign, jnp.int32).at[pos].set(
        jnp.arange(n_assign, dtype=jnp.int32), unique_indices=True)
    is_real = slot_assign < n_assign
    slot_tok = jnp.where(is_real, slot_assign // TOP_K, 0)
    weight_ext = jnp.concatenate([weight.reshape(n_assign), jnp.zeros((1,), F32)])
    slot_w = weight_ext[slot_assign]
    block_start = jnp.arange(n_blocks, dtype=jnp.int32) * MOE_BLOCK
    block_expert = jnp.minimum(jnp.sum((pad_end[None, :] <= block_start[:, None]).astype(jnp.int32), axis=1),
                               N_EXPERTS - 1)
    n_used = (pad_end[-1] // MOE_BLOCK).reshape(1)
    return block_expert, slot_tok, slot_w.reshape(n_slots, 1), pos, n_used, n_blocks


def _rope_tables():
    s = jnp.arange(SEQ)
    rows = (s // GRID_W).astype(F32)
    cols = (s % GRID_W).astype(F32)

    def angles(rot_dim):
        n = rot_dim // 4
        inv = ROPE_THETA ** (-jnp.arange(n, dtype=F32) / n)
        return jnp.concatenate([rows[:, None] * inv, cols[:, None] * inv], axis=-1)

    ang_a = angles(HD)
    cos_a, sin_a = jnp.cos(ang_a), jnp.sin(ang_a)
    tab_a = jnp.concatenate([cos_a, cos_a, -sin_a, sin_a], axis=-1)
    ident_a = jnp.concatenate([jnp.ones((T_CTX, HD), F32), jnp.zeros((T_CTX, HD), F32)], axis=-1)
    ang_c = angles(ROPE)
    cos_c, sin_c = jnp.cos(ang_c), jnp.sin(ang_c)
    half = ROPE // 2
    z_half = jnp.zeros((SEQ, half), F32)
    z_pad = jnp.zeros((SEQ, 128 - ROPE), F32)
    tab_c = jnp.concatenate([cos_c, cos_c, z_pad, -sin_c, z_half, z_pad, z_half, sin_c, z_pad], axis=-1)
    ident_c = jnp.concatenate([jnp.ones((T_CTX, ROPE), F32), jnp.zeros((T_CTX, 3 * 128 - ROPE), F32)], axis=-1)
    return jnp.concatenate([tab_a, ident_a], axis=0), jnp.concatenate([tab_c, ident_c], axis=0)


def _pad_head_cols(w, lead):
    w = w.reshape(lead, MLA_H, MLA_QK)
    return jnp.pad(w, ((0, 0), (0, 0), (0, MLA_PAD - MLA_QK))).reshape(lead, MLA_H * MLA_PAD)


def kernel(x, c, ctx, c_ctx, w_mod, b_mod, norm1, norm2, w_in, gqa_q_norm, gqa_k_norm, mla_q_lora_norm, w_uq,
           mla_kv_lora_norm, w_ukv, mla_q_norm, mla_k_norm, conv_w, w_out_a, w_out_b, w_out_c, b_gate, w_o,
           w_group, b_group, w_router, b_router, w_gu, w_down):
    xs = jnp.concatenate([x.reshape(T_LAT, D), ctx.reshape(T_CTX, D)], axis=0)
    cin = jnp.concatenate([c, c_ctx[None, :], jnp.zeros((3, D), F32)], axis=0)
    mod_all = _modulation(cin, w_mod, b_mod)
    rope_a, rope_c = _rope_tables()

    for l in range(DEPTH):
        last = l == DEPTH - 1
        n_rows = T_LAT if last else T_ALL
        mod = mod_all[l].reshape(8, 1, 6 * D)

        w_in_b = w_in[l].astype(BF16)
        w_kvq = jnp.concatenate(
            [w_in_b[:, :KV_COLS], jnp.zeros((D, 1024 - KV_COLS), BF16), w_in_b[:, KV_COLS:REST_OFF]], axis=1)
        w_rest = w_in_b[:, REST_OFF:]

        pkvq = _normed_matmul(xs, norm1[l], mod, 1, 0, w_kvq, T_ALL, KVQ_PAD // 2, "in_proj_kvq")
        prest = _normed_matmul(xs, norm1[l], mod, 1, 0, w_rest, n_rows, 1536, "in_proj_rest")

        ka, va, kc, vc, qa, qc = _attention_prep(
            pkvq, rope_a, rope_c,
            gqa_q_norm[l].reshape(1, HD), gqa_k_norm[l].reshape(1, HD),
            mla_q_lora_norm[l].reshape(1, Q_LORA), mla_kv_lora_norm[l].reshape(1, KV_LORA),
            jnp.pad(mla_q_norm[l], (0, MLA_PAD - MLA_QK)).reshape(1, MLA_PAD),
            jnp.pad(mla_k_norm[l], (0, MLA_PAD - MLA_QK)).reshape(1, MLA_PAD),
            _pad_head_cols(w_uq[l], Q_LORA).astype(BF16), w_ukv[l].astype(BF16))

        y_a = _attention(qa, ka, va, heads=KVH, group=GROUP, dk=HD, dv=HD, tq=ATTN_ROWS // GROUP, chunk=512,
                         with_ctx_queries=not last, name="gqa_attention")
        y_c = _attention(qc, kc, vc, heads=MLA_H, group=1, dk=MLA_PAD, dv=MLA_V, tq=ATTN_ROWS, chunk=512,
                         with_ctx_queries=not last, name="mla_attention")
        y_b = _conv_branch(prest, conv_w[l], n_rows)
        m = _merge(y_a, y_b, y_c, prest, b_gate[l].reshape(1, 3 * D),
                   w_out_a[l].astype(BF16), w_out_b[l].astype(BF16), w_out_c[l].astype(BF16), n_rows)
        xs = _oproj_residual(m, w_o[l].astype(BF16), xs, mod, 2, n_rows)

        w_r = jnp.concatenate([w_router[l], w_group[l], jnp.zeros((D, ROUTE_PAD - N_GROUPS - N_EXPERTS), F32)], axis=1).T
        w_r_hi = w_r.astype(BF16)
        w_r_lo = (w_r - w_r_hi.astype(F32)).astype(BF16)
        b_r = jnp.concatenate([b_router[l], b_group[l], jnp.zeros((ROUTE_PAD - N_GROUPS - N_EXPERTS,), F32)])
        h2, route, counts = _router(xs, norm2[l], mod, 4, 3, w_r_hi, w_r_lo, b_r.reshape(ROUTE_PAD, 1), n_rows)
        block_expert, slot_tok, slot_w, pos, n_used, n_blocks = _slot_layout(route, counts, n_rows)
        ys = _experts(block_expert, slot_tok, n_used, h2, w_gu, w_down, l, slot_w, n_blocks)
        xs = _combine(pos, ys, xs, mod, 5, n_rows)

    return xs[:T_LAT].reshape(BATCH, SEQ, D)
```

```python
import functools

import numpy as np
import jax
import jax.numpy as jnp
from jax import lax
from jax.experimental import pallas as pl
from jax.experimental.pallas import tpu as pltpu

D = 2048
BATCH = 4
SEQ = 4096
DEPTH = 2
GRID_W = 64
CTX = 256
ROPE_THETA = 10000.0
EPS = 1e-6
KVH = 2
GROUP = 4
HD = 128
CONV_DIM = 1024
MLA_H = 8
Q_LORA = 512
KV_LORA = 256
NOPE = 128
ROPE = 64
MLA_V = 128
MLA_QK = NOPE + ROPE
MLA_PAD = 256
N_GROUPS = 4
EPG = 8
N_EXPERTS = N_GROUPS * EPG
TOP_K = 2
D_EXPERT = 512
MOE_BLOCK = 256

T_LAT = BATCH * SEQ
T_CTX = BATCH * CTX
T_ALL = T_LAT + T_CTX
KV_COLS = 2 * KVH * HD + KV_LORA + ROPE
Q_COLS = KVH * GROUP * HD + Q_LORA
REST_OFF = KV_COLS + Q_COLS
KVQ_PAD = 1024 + Q_COLS
REST_COLS = 3 * CONV_DIM + 3 * D
ROUTE_PAD = 128

V7X_VMEM_LIMIT = 56 * 1024 * 1024
TM = 1024
ATTN_ROWS = 1024
BF16_SUBLANES = 16
LOG2_E = float(np.log2(np.e))

F32 = jnp.float32
BF16 = jnp.bfloat16


def _params(*sem):
    return pltpu.CompilerParams(dimension_semantics=sem, vmem_limit_bytes=V7X_VMEM_LIMIT)


def _dot(a, b):
    return jnp.dot(a, b, preferred_element_type=F32)


def _mod_kernel(c_ref, w_ref, b_ref, o_ref):
    c = c_ref[...]
    a = (c * jax.nn.sigmoid(c)).astype(BF16)
    o_ref[...] = _dot(a, w_ref[...].astype(BF16)) + b_ref[...]


def _modulation(cin, w_mod, b_mod):
    tn = 1024
    return pl.pallas_call(
        _mod_kernel,
        out_shape=jax.ShapeDtypeStruct((DEPTH, 8, 6 * D), F32),
        grid=(DEPTH, 6 * D // tn),
        in_specs=[
            pl.BlockSpec((8, D), lambda l, j: (0, 0)),
            pl.BlockSpec((None, D, tn), lambda l, j: (l, 0, j)),
            pl.BlockSpec((None, 1, tn), lambda l, j: (l, 0, j)),
        ],
        out_specs=pl.BlockSpec((None, 8, tn), lambda l, j: (l, 0, j)),
        compiler_params=_params("parallel", "parallel"),
        name="modulation",
    )(cin, w_mod, b_mod.reshape(DEPTH, 1, 6 * D))


def _norm_mod_rows(x, g, sc, sh):
    ms = jnp.mean(x * x, axis=-1, keepdims=True)
    return (x * lax.rsqrt(ms + EPS) * g) * (1.0 + sc) + sh


def _normed_matmul_kernel(x_ref, g_ref, sc_ref, sh_ref, w_ref, o_ref, h_scr):
    rows = 128

    @pl.when(pl.program_id(1) == 0)
    def _():
        def body(r, carry):
            sl = pl.ds(pl.multiple_of(r * rows, rows), rows)
            h_scr[sl, :] = _norm_mod_rows(x_ref[sl, :], g_ref[...], sc_ref[...], sh_ref[...]).astype(BF16)
            return carry

        lax.fori_loop(0, TM // rows, body, 0)

    o_ref[...] = _dot(h_scr[...], w_ref[...]).astype(o_ref.dtype)


def _normed_matmul(xs, gain, mod, sc_chunk, sh_chunk, w, layer, n_rows, tn, name):
    n = w.shape[2]
    return pl.pallas_call(
        _normed_matmul_kernel,
        out_shape=jax.ShapeDtypeStruct((n_rows, n), BF16),
        grid=(n_rows // TM, n // tn),
        in_specs=[
            pl.BlockSpec((TM, D), lambda i, j: (i, 0)),
            pl.BlockSpec((1, D), lambda i, j: (0, 0)),
            pl.BlockSpec((None, 1, D), lambda i, j: (i // (SEQ // TM), 0, sc_chunk)),
            pl.BlockSpec((None, 1, D), lambda i, j: (i // (SEQ // TM), 0, sh_chunk)),
            pl.BlockSpec((None, D, tn), lambda i, j: (layer, 0, j)),
        ],
        out_specs=pl.BlockSpec((TM, tn), lambda i, j: (i, j)),
        scratch_shapes=[pltpu.VMEM((TM, D), BF16)],
        compiler_params=_params("parallel", "arbitrary"),
        name=name,
    )(xs, gain.reshape(1, D), mod, mod, w)


def _prep_kernel(p_ref, ra_ref, rc_ref, gq_ref, gk_ref, gql_ref, gkl_ref, gmq_ref, gmk_ref, wuq_ref, wukv_ref,
                 ka_ref, va_ref, kc_ref, vc_ref, qa_ref, qc_ref):
    cos_a = ra_ref[:, 0:HD]
    sin_a = ra_ref[:, HD:2 * HD]
    cos_c = rc_ref[:, 0:128]
    sin_up = rc_ref[:, 128:256]
    sin_dn = rc_ref[:, 256:384]
    pair_ones = (lax.broadcasted_iota(jnp.int32, (256, 256), 0) // 128
                 == lax.broadcasted_iota(jnp.int32, (256, 256), 1) // 128).astype(BF16)

    def pair_sums(a, b):
        s = _dot(jnp.concatenate([a, b], axis=1).astype(BF16), pair_ones)
        return s[:, :128], s[:, 128:]

    def inv_rms(ss, width):
        return lax.rsqrt(ss * (1.0 / width) + EPS)

    def rope_a(x):
        return x * cos_a + pltpu.roll(x, HD // 2, 1) * sin_a

    def rope_c(x):
        return x * cos_c + pltpu.roll(x, 128 - ROPE // 2, 1) * sin_up + pltpu.roll(x, ROPE // 2, 1) * sin_dn

    def slab(col):
        return p_ref[:, col:col + 128].astype(F32)

    k0, k1 = slab(0), slab(HD)
    ss0, ss1 = pair_sums(k0 * k0, k1 * k1)
    ka_ref[:, 0:HD] = rope_a(k0 * inv_rms(ss0, HD) * gk_ref[...]).astype(BF16)
    ka_ref[:, HD:2 * HD] = rope_a(k1 * inv_rms(ss1, HD) * gk_ref[...]).astype(BF16)
    ones = jnp.ones((p_ref.shape[0], 128), BF16)
    for h in range(KVH):
        va_ref[:, 2 * h * HD:(2 * h + 1) * HD] = p_ref[:, (KVH + h) * HD:(KVH + h + 1) * HD]
        va_ref[:, (2 * h + 1) * HD:(2 * h + 2) * HD] = ones

    c0, c1 = slab(512), slab(640)
    krope = slab(768)
    ss_c, ss_kr = pair_sums(c0 * c0 + c1 * c1, krope * krope)
    r_c = inv_rms(ss_c, KV_LORA)
    ckv_n = jnp.concatenate([c0 * r_c * gkl_ref[:, 0:128], c1 * r_c * gkl_ref[:, 128:256]], axis=1).astype(BF16)
    kv = _dot(ckv_n, wukv_ref[...])
    krope_rot = rope_c(krope * gmk_ref[:, 128:256])
    for h2 in range(MLA_H // 2):
        kn = [kv[:, (2 * h2 + j) * 256:(2 * h2 + j) * 256 + NOPE] for j in range(2)]
        ss = pair_sums(kn[0] * kn[0], kn[1] * kn[1])
        for j in range(2):
            h = 2 * h2 + j
            r = inv_rms(ss[j] + ss_kr, MLA_QK)
            kc_ref[:, h * MLA_PAD:h * MLA_PAD + NOPE] = (kn[j] * r * gmk_ref[:, 0:128]).astype(BF16)
            kc_ref[:, h * MLA_PAD + NOPE:(h + 1) * MLA_PAD] = (krope_rot * r).astype(BF16)
            vc_ref[:, 2 * h * MLA_V:(2 * h + 1) * MLA_V] = kv[:, h * 256 + NOPE:(h + 1) * 256].astype(BF16)
            vc_ref[:, (2 * h + 1) * MLA_V:(2 * h + 2) * MLA_V] = ones

    for h2 in range(KVH * GROUP // 2):
        q = [slab(1024 + (2 * h2 + j) * HD) for j in range(2)]
        ss = pair_sums(q[0] * q[0], q[1] * q[1])
        for j in range(2):
            h = 2 * h2 + j
            qa_ref[:, h * HD:(h + 1) * HD] = (rope_a(q[j] * inv_rms(ss[j], HD) * gq_ref[...])
                                              * (HD ** -0.5 * LOG2_E)).astype(BF16)

    cq = [slab(2048 + j * 128) for j in range(Q_LORA // 128)]
    sq = functools.reduce(lambda a, b: a + b, [c * c for c in cq])
    r_q = inv_rms(pair_sums(sq, sq)[0], Q_LORA)
    cq_n = jnp.concatenate([c * r_q * gql_ref[:, j * 128:(j + 1) * 128] for j, c in enumerate(cq)], axis=1).astype(BF16)
    qc = _dot(cq_n, wuq_ref[...])
    for h2 in range(MLA_H // 2):
        qn = [qc[:, (2 * h2 + j) * MLA_PAD:(2 * h2 + j) * MLA_PAD + NOPE] for j in range(2)]
        qr = [qc[:, (2 * h2 + j) * MLA_PAD + NOPE:(2 * h2 + j + 1) * MLA_PAD] for j in range(2)]
        ss = pair_sums(qn[0] * qn[0] + qr[0] * qr[0], qn[1] * qn[1] + qr[1] * qr[1])
        for j in range(2):
            h = 2 * h2 + j
            r = inv_rms(ss[j], MLA_QK) * (MLA_QK ** -0.5 * LOG2_E)
            qc_ref[:, h * MLA_PAD:h * MLA_PAD + NOPE] = (qn[j] * r * gmq_ref[:, 0:128]).astype(BF16)
            qc_ref[:, h * MLA_PAD + NOPE:(h + 1) * MLA_PAD] = (rope_c(qr[j] * gmq_ref[:, 128:256]) * r).astype(BF16)


def _attention_prep(pkvq, rope_a, rope_c, gq, gk, gql, gkl, gmq, gmk, wuq, wukv):
    tm = 256
    n_lat = T_LAT // tm
    per_seq = SEQ // tm

    def rope_idx(i):
        return jnp.where(i < n_lat, i % per_seq, per_seq + i - n_lat)

    def full(shape):
        return pl.BlockSpec(shape, lambda i: (0, 0))

    def rows(width):
        return pl.BlockSpec((tm, width), lambda i: (i, 0))

    out_widths = (KVH * HD, 2 * KVH * HD, MLA_H * MLA_PAD, 2 * MLA_H * MLA_V, KVH * GROUP * HD, MLA_H * MLA_PAD)
    return pl.pallas_call(
        _prep_kernel,
        out_shape=[jax.ShapeDtypeStruct((T_ALL, w), BF16) for w in out_widths],
        grid=(T_ALL // tm,),
        in_specs=[
            rows(KVQ_PAD),
            pl.BlockSpec((tm, 2 * HD), lambda i: (rope_idx(i), 0)),
            pl.BlockSpec((tm, 3 * 128), lambda i: (rope_idx(i), 0)),
            full((1, HD)), full((1, HD)), full((1, Q_LORA)), full((1, KV_LORA)),
            full((1, MLA_PAD)), full((1, MLA_PAD)),
            full((Q_LORA, MLA_H * MLA_PAD)), full((KV_LORA, MLA_H * (NOPE + MLA_V))),
        ],
        out_specs=[rows(w) for w in out_widths],
        compiler_params=_params("parallel"),
        name="attention_prep",
    )(pkvq, rope_a, rope_c, gq, gk, gql, gkl, gmq, gmk, wuq, wukv)


def _attn_kernel(q_ref, *refs, group, dk, dv, chunk):
    segments = [(refs[2 * i], refs[2 * i + 1]) for i in range((len(refs) - 1) // 2)]
    o_ref = refs[-1]
    tq = q_ref.shape[0]
    if group > 1:
        q = jnp.concatenate([q_ref[:, g * dk:(g + 1) * dk] for g in range(group)], axis=0)
    else:
        q = q_ref[...]
    m = None
    for k_ref, v_ref in segments:
        n_keys = k_ref.shape[0]
        step = min(chunk, n_keys)
        for c0 in range(0, n_keys, step):
            s = lax.dot_general(q, k_ref[c0:c0 + step, :], (((1,), (1,)), ((), ())), preferred_element_type=F32)
            c_max = jnp.max(s, axis=-1, keepdims=True)
            if m is None:
                m = c_max
                acc = _dot(jnp.exp2(s - m).astype(BF16), v_ref[c0:c0 + step, :])
            else:
                m_new = jnp.maximum(m, c_max)
                acc = jnp.exp2(m - m_new) * acc + _dot(jnp.exp2(s - m_new).astype(BF16), v_ref[c0:c0 + step, :])
                m = m_new
    o = acc[:, :dv] / acc[:, dv:]
    for g in range(group):
        o_ref[:, g * dv:(g + 1) * dv] = o[g * tq:(g + 1) * tq].astype(o_ref.dtype)


def _attention_latent(q, k, v, *, heads, group, dk, dv, tq, chunk, name):
    n_qt = SEQ // tq
    kern = functools.partial(_attn_kernel, group=group, dk=dk, dv=dv, chunk=chunk)
    return pl.pallas_call(
        kern,
        out_shape=jax.ShapeDtypeStruct((T_LAT, heads * group * dv), BF16),
        grid=(BATCH, heads, n_qt),
        in_specs=[
            pl.BlockSpec((tq, group * dk), lambda b, h, qi: (b * n_qt + qi, h)),
            pl.BlockSpec((CTX, dk), lambda b, h, qi: (T_LAT // CTX + b, h)),
            pl.BlockSpec((CTX, 2 * dv), lambda b, h, qi: (T_LAT // CTX + b, h)),
            pl.BlockSpec((SEQ, dk), lambda b, h, qi: (b, h)),
            pl.BlockSpec((SEQ, 2 * dv), lambda b, h, qi: (b, h)),
        ],
        out_specs=pl.BlockSpec((tq, group * dv), lambda b, h, qi: (b * n_qt + qi, h)),
        compiler_params=_params("parallel", "parallel", "arbitrary"),
        name=name,
    )(q, k, v, k, v)


def _attention_context(q, k, v, *, heads, group, dk, dv, name):
    kern = functools.partial(_attn_kernel, group=group, dk=dk, dv=dv, chunk=CTX)
    ctx0 = T_LAT // CTX
    return pl.pallas_call(
        kern,
        out_shape=jax.ShapeDtypeStruct((T_CTX, heads * group * dv), BF16),
        grid=(BATCH, heads),
        in_specs=[
            pl.BlockSpec((CTX, group * dk), lambda b, h: (ctx0 + b, h)),
            pl.BlockSpec((CTX, dk), lambda b, h: (ctx0 + b, h)),
            pl.BlockSpec((CTX, 2 * dv), lambda b, h: (ctx0 + b, h)),
        ],
        out_specs=pl.BlockSpec((CTX, group * dv), lambda b, h: (b, h)),
        compiler_params=_params("parallel", "parallel"),
        name=name,
    )(q, k, v)


def _attention(q, k, v, *, heads, group, dk, dv, tq, chunk, with_ctx_queries, name):
    y = _attention_latent(q, k, v, heads=heads, group=group, dk=dk, dv=dv, tq=tq, chunk=chunk, name=name)
    if not with_ctx_queries:
        return y, y[:T_CTX]
    return y, _attention_context(q, k, v, heads=heads, group=group, dk=dk, dv=dv, name=name + "_ctx")


def _conv_kernel(xb_ref, gb_ref, gc_ref, xbp_ref, gcp_ref, xbn_ref, gcn_ref, w_ref, o_ref):
    i = pl.program_id(0)
    z = gc_ref[...].astype(F32) * xb_ref[...].astype(F32)
    last = BF16_SUBLANES - 1
    z_prev = gcp_ref[last:last + 1, :].astype(F32) * xbp_ref[last:last + 1, :].astype(F32)
    z_next = gcn_ref[0:1, :].astype(F32) * xbn_ref[0:1, :].astype(F32)
    row = lax.broadcasted_iota(jnp.int32, (TM, 1), 0)
    seq_len = jnp.where(i < T_LAT // TM, SEQ, CTX)
    pos = (i * TM + row) & (seq_len - 1)
    z_dn = jnp.where(row == 0, z_prev, pltpu.roll(z, 1, 0))
    z_up = jnp.where(row == TM - 1, z_next, pltpu.roll(z, TM - 1, 0))
    z_dn = jnp.where(pos == 0, 0.0, z_dn)
    z_up = jnp.where(pos == seq_len - 1, 0.0, z_up)
    y = gb_ref[...].astype(F32) * (w_ref[0:1, :] * z_dn + w_ref[1:2, :] * z + w_ref[2:3, :] * z_up)
    o_ref[...] = y.astype(BF16)


def _conv_branch(prest, conv_w, n_rows):
    tc = 256
    n_ct = CONV_DIM // tc
    halo = TM // BF16_SUBLANES
    last_halo = n_rows // BF16_SUBLANES - 1

    def main(col0):
        return pl.BlockSpec((TM, tc), lambda i, j: (i, col0 * n_ct + j))

    def prev(col0):
        return pl.BlockSpec((BF16_SUBLANES, tc), lambda i, j: (jnp.maximum(i * halo - 1, 0), col0 * n_ct + j))

    def nxt(col0):
        return pl.BlockSpec((BF16_SUBLANES, tc), lambda i, j: (jnp.minimum((i + 1) * halo, last_halo), col0 * n_ct + j))

    return pl.pallas_call(
        _conv_kernel,
        out_shape=jax.ShapeDtypeStruct((n_rows, CONV_DIM), BF16),
        grid=(n_rows // TM, n_ct),
        in_specs=[main(0), main(1), main(2), prev(0), prev(2), nxt(0), nxt(2),
                  pl.BlockSpec((3, tc), lambda i, j: (0, j))],
        out_specs=pl.BlockSpec((TM, tc), lambda i, j: (i, j)),
        compiler_params=_params("parallel", "parallel"),
        name="conv_branch",
    )(prest, prest, prest, prest, prest, prest, prest, conv_w)


def _merge_kernel(ya_ref, yax_ref, yb_ref, yc_ref, ycx_ref, ga_ref, gb_ref, gc_ref, ba_ref, bb_ref, bc_ref,
                  wa_ref, wb_ref, wc_ref, o_ref):
    is_ctx = pl.program_id(0) == T_LAT // TM

    def branch(y, w_ref, g_ref, b_ref):
        return jax.nn.sigmoid(g_ref[...].astype(F32) + b_ref[...]) * _dot(y, w_ref[...])

    m = branch(jnp.where(is_ctx, yax_ref[...], ya_ref[...]), wa_ref, ga_ref, ba_ref)
    m = m + branch(yb_ref[...], wb_ref, gb_ref, bb_ref)
    m = m + branch(jnp.where(is_ctx, ycx_ref[...], yc_ref[...]), wc_ref, gc_ref, bc_ref)
    o_ref[...] = m.astype(BF16)


def _merge(ya, ya_ctx, yb, yc, yc_ctx, prest, b_gate, wa, wb, wc, layer, n_rows):
    tn = 512
    n_ct = D // tn
    gate_blk0 = 3 * CONV_DIM // tn
    last_lat = T_LAT // TM - 1

    def y_spec():
        return pl.BlockSpec((TM, CONV_DIM), lambda i, j: (i, 0))

    def y_lat_spec():
        return pl.BlockSpec((TM, CONV_DIM), lambda i, j: (jnp.minimum(i, last_lat), 0))

    def y_ctx_spec():
        return pl.BlockSpec((TM, CONV_DIM), lambda i, j: (0, 0))

    def gate_spec(k):
        return pl.BlockSpec((TM, tn), lambda i, j: (i, gate_blk0 + k * n_ct + j))

    def bias_spec(k):
        return pl.BlockSpec((1, tn), lambda i, j: (0, k * n_ct + j))

    def w_spec():
        return pl.BlockSpec((None, CONV_DIM, tn), lambda i, j: (layer, 0, j))

    return pl.pallas_call(
        _merge_kernel,
        out_shape=jax.ShapeDtypeStruct((n_rows, D), BF16),
        grid=(n_rows // TM, n_ct),
        in_specs=[y_lat_spec(), y_ctx_spec(), y_spec(), y_lat_spec(), y_ctx_spec(),
                  gate_spec(0), gate_spec(1), gate_spec(2),
                  bias_spec(0), bias_spec(1), bias_spec(2), w_spec(), w_spec(), w_spec()],
        out_specs=pl.BlockSpec((TM, tn), lambda i, j: (i, j)),
        compiler_params=_params("parallel", "arbitrary"),
        name="merge_branches",
    )(ya, ya_ctx, yb, yc, yc_ctx, prest, prest, prest, b_gate, b_gate, b_gate, wa, wb, wc)


def _oproj_kernel(m_ref, w_ref, x_ref, g_ref, o_ref):
    o_ref[...] = x_ref[...] + g_ref[...] * _dot(m_ref[...], w_ref[...])


def _oproj_residual(m, w_o, layer, xs, mod, gate_chunk, n_rows):
    tn = 512
    n_ct = D // tn
    return pl.pallas_call(
        _oproj_kernel,
        out_shape=jax.ShapeDtypeStruct((n_rows, D), F32),
        grid=(n_rows // TM, n_ct),
        in_specs=[
            pl.BlockSpec((TM, D), lambda i, j: (i, 0)),
            pl.BlockSpec((None, D, tn), lambda i, j: (layer, 0, j)),
            pl.BlockSpec((TM, tn), lambda i, j: (i, j)),
            pl.BlockSpec((None, 1, tn), lambda i, j: (i // (SEQ // TM), 0, gate_chunk * n_ct + j)),
        ],
        out_specs=pl.BlockSpec((TM, tn), lambda i, j: (i, j)),
        compiler_params=_params("parallel", "arbitrary"),
        name="oproj_residual",
    )(m, w_o, xs, mod)


def _router_kernel(x_ref, g_ref, sc_ref, sh_ref, whi_ref, wlo_ref, b_ref, h_ref, route_ref, cnt_ref, cnt_scr):
    cols = 128

    @pl.when(pl.program_id(0) == 0)
    def _():
        cnt_scr[...] = jnp.zeros_like(cnt_scr)

    tri = (lax.broadcasted_iota(jnp.int32, (cols, cols), 0)
           <= lax.broadcasted_iota(jnp.int32, (cols, cols), 1)).astype(BF16)
    sub = lax.broadcasted_iota(jnp.int32, (EPG, cols), 0).astype(F32)
    e_row = lax.broadcasted_iota(jnp.int32, (N_EXPERTS, cols), 0).astype(F32)
    nt_dims = (((1,), (1,)), ((), ()))
    neg = -jnp.inf

    def body(r, carry):
        sl = pl.ds(pl.multiple_of(r * cols, cols), cols)
        h = _norm_mod_rows(x_ref[sl, :], g_ref[...], sc_ref[...], sh_ref[...])
        for c in range(D // 128):
            h_ref[sl, c, :] = h[:, c * 128:(c + 1) * 128]
        h_hi = h.astype(BF16)
        h_lo = (h - h_hi.astype(F32)).astype(BF16)

        def wt_dot(w_ref, act):
            return lax.dot_general(w_ref[...], act, nt_dims, preferred_element_type=F32)

        lg = wt_dot(whi_ref, h_hi) + (wt_dot(wlo_ref, h_hi) + wt_dot(whi_ref, h_lo)) + b_ref[...]

        g_logit = [lg[N_EXPERTS + g:N_EXPERTS + g + 1, :] for g in range(N_GROUPS)]
        g_max = functools.reduce(jnp.maximum, g_logit)
        g_prob = 1.0 / functools.reduce(lambda a, b: a + b, [jnp.exp(v - g_max) for v in g_logit])
        g_idx = jnp.full_like(g_max, float(N_GROUPS - 1))
        for g in range(N_GROUPS - 2, -1, -1):
            g_idx = jnp.where(g_logit[g] == g_max, float(g), g_idx)

        cand = lg[(N_GROUPS - 1) * EPG:N_GROUPS * EPG, :]
        for g in range(N_GROUPS - 2, -1, -1):
            cand = jnp.where(g_idx == float(g), lg[g * EPG:(g + 1) * EPG, :], cand)
        m0 = jnp.max(cand, axis=0, keepdims=True)
        i0 = jnp.min(jnp.where(cand == m0, sub, float(EPG)), axis=0, keepdims=True)
        cand = jnp.where(sub == i0, neg, cand)
        m1 = jnp.max(cand, axis=0, keepdims=True)
        i1 = jnp.min(jnp.where(cand == m1, sub, float(EPG)), axis=0, keepdims=True)
        t = jnp.exp(m1 - m0)
        w0 = g_prob / (1.0 + t)
        w1 = g_prob * t / (1.0 + t)
        e0 = g_idx * EPG + i0
        e1 = g_idx * EPG + i1

        hit0 = e_row == e0
        hit1 = e_row == e1
        hits = jnp.where(hit0 | hit1, 1.0, 0.0)
        prefix = _dot(hits.astype(BF16), tri)
        before = prefix - hits + cnt_scr[...]
        rank0 = jnp.sum(jnp.where(hit0, before, 0.0), axis=0, keepdims=True)
        rank1 = jnp.sum(jnp.where(hit1, before, 0.0), axis=0, keepdims=True)
        cnt_scr[...] = cnt_scr[...] + prefix[:, cols - 1:cols]
        route_ref[:, sl] = jnp.concatenate([e0, e1, w0, w1, rank0, rank1, jnp.zeros((2, cols), F32)], axis=0)
        return carry

    lax.fori_loop(0, TM // cols, body, 0, unroll=2)
    cnt_ref[...] = cnt_scr[...]


def _router(xs, gain, mod, sc_chunk, sh_chunk, w_hi, w_lo, bias, n_rows):
    return pl.pallas_call(
        _router_kernel,
        out_shape=[jax.ShapeDtypeStruct((n_rows, D // 128, 128), F32), jax.ShapeDtypeStruct((8, n_rows), F32),
                   jax.ShapeDtypeStruct((N_EXPERTS, 1), F32)],
        grid=(n_rows // TM,),
        in_specs=[
            pl.BlockSpec((TM, D), lambda i: (i, 0)),
            pl.BlockSpec((1, D), lambda i: (0, 0)),
            pl.BlockSpec((None, 1, D), lambda i: (i // (SEQ // TM), 0, sc_chunk)),
            pl.BlockSpec((None, 1, D), lambda i: (i // (SEQ // TM), 0, sh_chunk)),
            pl.BlockSpec((ROUTE_PAD, D), lambda i: (0, 0)),
            pl.BlockSpec((ROUTE_PAD, D), lambda i: (0, 0)),
            pl.BlockSpec((ROUTE_PAD, 1), lambda i: (0, 0)),
        ],
        out_specs=[pl.BlockSpec((TM, D // 128, 128), lambda i: (i, 0, 0)), pl.BlockSpec((8, TM), lambda i: (0, i)),
                   pl.BlockSpec((N_EXPERTS, 1), lambda i: (0, 0))],
        scratch_shapes=[pltpu.VMEM((N_EXPERTS, 1), F32)],
        compiler_params=_params("arbitrary"),
        name="moe_router",
    )(xs, gain.reshape(1, D), mod, mod, w_hi, w_lo, bias)


def _slab_copy(src_hbm, token, dst, dst_row, sem):
    return pltpu.make_async_copy(src_hbm.at[token], dst.at[:, dst_row, :], sem)


def _start_slab_gather(src_hbm, dst, sem, n, index_of):
    def issue(r, carry):
        _slab_copy(src_hbm, index_of(r), dst, r, sem).start()
        return carry

    lax.fori_loop(0, n, issue, 0, unroll=8)


def _wait_slab_gather(src_hbm, dst, sem, n):
    def drain(r, carry):
        _slab_copy(src_hbm, 0, dst, r, sem).wait()
        return carry

    lax.fori_loop(0, n, drain, 0, unroll=8)


EXPERT_K_SPLIT = 8
N_CHUNKS = D // 128


def _expert_kernel(be_ref, tok_ref, nused_ref, h_hbm, wgu_ref, wdn_ref, sw_ref, ys_hbm,
                   xbuf0, xbuf1, ybuf0, ybuf1, zbuf, sem_in, sem_out):
    i = pl.program_id(0)
    n_used = nused_ref[0]
    xbufs = (xbuf0, xbuf1)
    ybufs = (ybuf0, ybuf1)
    rows_per_group = MOE_BLOCK // EXPERT_K_SPLIT
    chunks_per_group = N_CHUNKS // EXPERT_K_SPLIT
    kc = D // EXPERT_K_SPLIT

    def issue(blk, s, r0, r1):
        for r in range(r0, r1):
            _slab_copy(h_hbm, tok_ref[blk * MOE_BLOCK + r], xbufs[s], r, sem_in.at[s]).start()

    def out_copy(blk, s, c, src=None):
        src = ybufs[s] if src is None else src
        return pltpu.make_async_copy(src.at[c], ys_hbm.at[pl.ds(blk * MOE_BLOCK, MOE_BLOCK), c, :], sem_out.at[s])

    @pl.when(i == 0)
    def _():
        issue(0, 0, 0, MOE_BLOCK)
        zbuf[...] = jnp.zeros_like(zbuf)

    for s in range(2):
        mine = (i & 1) == s

        @pl.when(mine & (i <= n_used))
        def _():
            _wait_slab_gather(h_hbm, xbufs[s], sem_in.at[s], MOE_BLOCK)

        @pl.when(mine & (i >= 2) & (i < n_used + 2))
        def _():
            for c in range(N_CHUNKS):
                out_copy(0, s, c).wait()

        @pl.when(mine & (i < n_used))
        def _():
            gu = None
            for kk in range(EXPERT_K_SPLIT):
                issue(i + 1, 1 - s, kk * rows_per_group, (kk + 1) * rows_per_group)
                xk = jnp.concatenate([xbufs[s][kk * chunks_per_group + c] for c in range(chunks_per_group)], axis=1)
                part = _dot(xk.astype(BF16), wgu_ref[kk * kc:(kk + 1) * kc, :].astype(BF16))
                gu = part if gu is None else gu + part
            gate = gu[:, :D_EXPERT]
            act = gate * jax.nn.sigmoid(gate) * gu[:, D_EXPERT:]
            y = _dot(act.astype(BF16), wdn_ref[...].astype(BF16)) * sw_ref[...]
            for c in range(N_CHUNKS):
                ybufs[s][c] = y[:, c * 128:(c + 1) * 128]
            for c in range(N_CHUNKS):
                out_copy(i, s, c).start()

        @pl.when(mine & (i >= n_used))
        def _():
            for c in range(N_CHUNKS):
                out_copy(i, s, c, zbuf).start()
            for c in range(N_CHUNKS):
                out_copy(i, s, c, zbuf).wait()


def _experts(block_expert, slot_tok, n_used, h2, w_gu, w_down, layer, slot_w, n_blocks):
    grid_spec = pltpu.PrefetchScalarGridSpec(
        num_scalar_prefetch=3,
        grid=(n_blocks,),
        in_specs=[
            pl.BlockSpec(memory_space=pl.ANY),
            pl.BlockSpec((None, None, D, 2 * D_EXPERT), lambda i, be, tok, nu: (layer, be[i], 0, 0)),
            pl.BlockSpec((None, None, D_EXPERT, D), lambda i, be, tok, nu: (layer, be[i], 0, 0)),
            pl.BlockSpec((MOE_BLOCK, 1), lambda i, be, tok, nu: (i, 0)),
        ],
        out_specs=pl.BlockSpec(memory_space=pl.ANY),
        scratch_shapes=[pltpu.VMEM((N_CHUNKS, MOE_BLOCK, 128), F32) for _ in range(5)]
        + [pltpu.SemaphoreType.DMA((2,)), pltpu.SemaphoreType.DMA((2,))],
    )
    return pl.pallas_call(
        _expert_kernel,
        out_shape=jax.ShapeDtypeStruct((n_blocks * MOE_BLOCK, N_CHUNKS, 128), F32),
        grid_spec=grid_spec,
        compiler_params=_params("arbitrary"),
        name="moe_experts",
    )(block_expert, slot_tok, n_used, h2, w_gu, w_down, slot_w)


def _combine_kernel(pos_ref, ys_hbm, x_ref, g_ref, o_ref, buf, sem):
    i = pl.program_id(0)
    n_tiles = pl.num_programs(0)
    rows = x_ref.shape[0]
    slot = i & 1

    def start(tile, s):
        _start_slab_gather(ys_hbm, buf.at[s], sem.at[s], TOP_K * rows,
                           lambda r: pos_ref[(tile * rows + (r & (rows - 1))) * TOP_K + (r >> int(np.log2(rows)))])

    @pl.when(i == 0)
    def _():
        start(0, 0)

    _wait_slab_gather(ys_hbm, buf.at[slot], sem.at[slot], TOP_K * rows)

    @pl.when(i + 1 < n_tiles)
    def _():
        start(i + 1, 1 - slot)

    for c in range(N_CHUNKS):
        lanes = slice(c * 128, (c + 1) * 128)
        o_ref[:, lanes] = x_ref[:, lanes] + g_ref[:, lanes] * (buf[slot, c, 0:rows, :] + buf[slot, c, rows:2 * rows, :])


def _combine(pos, ys, xs, mod, gate_chunk, n_rows):
    rows = 256
    grid_spec = pltpu.PrefetchScalarGridSpec(
        num_scalar_prefetch=1,
        grid=(n_rows // rows,),
        in_specs=[
            pl.BlockSpec(memory_space=pl.ANY),
            pl.BlockSpec((rows, D), lambda i, pos: (i, 0)),
            pl.BlockSpec((None, 1, D), lambda i, pos: (i // (SEQ // rows), 0, gate_chunk)),
        ],
        out_specs=pl.BlockSpec((rows, D), lambda i, pos: (i, 0)),
        scratch_shapes=[pltpu.VMEM((2, N_CHUNKS, TOP_K * rows, 128), F32), pltpu.SemaphoreType.DMA((2,))],
    )
    return pl.pallas_call(
        _combine_kernel,
        out_shape=jax.ShapeDtypeStruct((n_rows, D), F32),
        grid_spec=grid_spec,
        compiler_params=_params("arbitrary"),
        name="moe_combine",
    )(pos, ys, xs, mod)


def _slot_layout(route, counts, n_tok):
    n_assign = n_tok * TOP_K
    expert = route[0:TOP_K, :].T.astype(jnp.int32)
    weight = route[TOP_K:2 * TOP_K, :].T
    rank = route[2 * TOP_K:3 * TOP_K, :].T.astype(jnp.int32)
    sizes = counts[:, 0].astype(jnp.int32)
    padded = (sizes + MOE_BLOCK - 1) // MOE_BLOCK * MOE_BLOCK
    pad_end = jnp.cumsum(padded)
    pad_start = pad_end - padded
    pos = (pad_start[expert] + rank).reshape(n_assign)
    n_blocks = (n_assign + MOE_BLOCK - 1) // MOE_BLOCK + N_EXPERTS + 2
    n_slots = n_blocks * MOE_BLOCK
    slot_assign = jnp.full((n_slots,), n_assign, jnp.int32).at[pos].set(
        jnp.arange(n_assign, dtype=jnp.int32), unique_indices=True)
    is_real = slot_assign < n_assign
    slot_tok = jnp.where(is_real, slot_assign // TOP_K, 0)
    weight_ext = jnp.concatenate([weight.reshape(n_assign), jnp.zeros((1,), F32)])
    slot_w = weight_ext[slot_assign]
    block_start = jnp.arange(n_blocks, dtype=jnp.int32) * MOE_BLOCK
    block_expert = jnp.minimum(jnp.sum((pad_end[None, :] <= block_start[:, None]).astype(jnp.int32), axis=1),
                               N_EXPERTS - 1)
    n_used = (pad_end[-1] // MOE_BLOCK).reshape(1)
    return block_expert, slot_tok, slot_w.reshape(n_slots, 1), pos, n_used, n_blocks


def _rope_tables():
    s = jnp.arange(SEQ)
    rows = (s // GRID_W).astype(F32)
    cols = (s % GRID_W).astype(F32)

    def angles(rot_dim):
        n = rot_dim // 4
        inv = ROPE_THETA ** (-jnp.arange(n, dtype=F32) / n)
        return jnp.concatenate([rows[:, None] * inv, cols[:, None] * inv], axis=-1)

    ang_a = angles(HD)
    cos_a, sin_a = jnp.cos(ang_a), jnp.sin(ang_a)
    tab_a = jnp.concatenate([cos_a, cos_a, -sin_a, sin_a], axis=-1)
    ident_a = jnp.concatenate([jnp.ones((T_CTX, HD), F32), jnp.zeros((T_CTX, HD), F32)], axis=-1)
    ang_c = angles(ROPE)
    cos_c, sin_c = jnp.cos(ang_c), jnp.sin(ang_c)
    half = ROPE // 2
    z_half = jnp.zeros((SEQ, half), F32)
    z_pad = jnp.zeros((SEQ, 128 - ROPE), F32)
    tab_c = jnp.concatenate([cos_c, cos_c, z_pad, -sin_c, z_half, z_pad, z_half, sin_c, z_pad], axis=-1)
    ident_c = jnp.concatenate([jnp.ones((T_CTX, ROPE), F32), jnp.zeros((T_CTX, 3 * 128 - ROPE), F32)], axis=-1)
    return jnp.concatenate([tab_a, ident_a], axis=0), jnp.concatenate([tab_c, ident_c], axis=0)


def _pad_head_cols(w, lead):
    w = w.reshape(lead, MLA_H, MLA_QK)
    return jnp.pad(w, ((0, 0), (0, 0), (0, MLA_PAD - MLA_QK))).reshape(lead, MLA_H * MLA_PAD)


def kernel(x, c, ctx, c_ctx, w_mod, b_mod, norm1, norm2, w_in, gqa_q_norm, gqa_k_norm, mla_q_lora_norm, w_uq,
           mla_kv_lora_norm, w_ukv, mla_q_norm, mla_k_norm, conv_w, w_out_a, w_out_b, w_out_c, b_gate, w_o,
           w_group, b_group, w_router, b_router, w_gu, w_down):
    xs = jnp.concatenate([x.reshape(T_LAT, D), ctx.reshape(T_CTX, D)], axis=0)
    cin = jnp.concatenate([c, c_ctx[None, :], jnp.zeros((3, D), F32)], axis=0)
    mod_all = _modulation(cin, w_mod, b_mod)
    rope_a, rope_c = _rope_tables()

    w_in_b = w_in.astype(BF16)
    w_kvq = jnp.concatenate([w_in_b[:, :, :KV_COLS], jnp.zeros((DEPTH, D, 1024 - KV_COLS), BF16),
                             w_in_b[:, :, KV_COLS:REST_OFF]], axis=2)
    w_rest = w_in_b[:, :, REST_OFF:]
    w_a, w_b, w_c, w_o_b = (w.astype(BF16) for w in (w_out_a, w_out_b, w_out_c, w_o))

    for l in range(DEPTH):
        last = l == DEPTH - 1
        n_rows = T_LAT if last else T_ALL
        mod = mod_all[l].reshape(8, 1, 6 * D)

        pkvq = _normed_matmul(xs, norm1[l], mod, 1, 0, w_kvq, l, T_ALL, KVQ_PAD // 2, "in_proj_kvq")
        prest = _normed_matmul(xs, norm1[l], mod, 1, 0, w_rest, l, n_rows, 1536, "in_proj_rest")

        ka, va, kc, vc, qa, qc = _attention_prep(
            pkvq, rope_a, rope_c,
            gqa_q_norm[l].reshape(1, HD), gqa_k_norm[l].reshape(1, HD),
            mla_q_lora_norm[l].reshape(1, Q_LORA), mla_kv_lora_norm[l].reshape(1, KV_LORA),
            jnp.pad(mla_q_norm[l], (0, MLA_PAD - MLA_QK)).reshape(1, MLA_PAD),
            jnp.pad(mla_k_norm[l], (0, MLA_PAD - MLA_QK)).reshape(1, MLA_PAD),
            _pad_head_cols(w_uq[l], Q_LORA).astype(BF16), w_ukv[l].astype(BF16))

        y_a, y_a_ctx = _attention(qa, ka, va, heads=KVH, group=GROUP, dk=HD, dv=HD, tq=ATTN_ROWS // GROUP, chunk=512,
                         with_ctx_queries=not last, name="gqa_attention")
        y_c, y_c_ctx = _attention(qc, kc, vc, heads=MLA_H, group=1, dk=MLA_PAD, dv=MLA_V, tq=ATTN_ROWS, chunk=512,
                         with_ctx_queries=not last, name="mla_attention")
        y_b = _conv_branch(prest, conv_w[l], n_rows)
        m = _merge(y_a, y_a_ctx, y_b, y_c, y_c_ctx, prest, b_gate[l].reshape(1, 3 * D), w_a, w_b, w_c, l, n_rows)
        xs = _oproj_residual(m, w_o_b, l, xs, mod, 2, n_rows)

        w_r = jnp.concatenate([w_router[l], w_group[l], jnp.zeros((D, ROUTE_PAD - N_GROUPS - N_EXPERTS), F32)], axis=1).T
        w_r_hi = w_r.astype(BF16)
        w_r_lo = (w_r - w_r_hi.astype(F32)).astype(BF16)
        b_r = jnp.concatenate([b_router[l], b_group[l], jnp.zeros((ROUTE_PAD - N_GROUPS - N_EXPERTS,), F32)])
        h2, route, counts = _router(xs, norm2[l], mod, 4, 3, w_r_hi, w_r_lo, b_r.reshape(ROUTE_PAD, 1), n_rows)
        block_expert, slot_tok, slot_w, pos, n_used, n_blocks = _slot_layout(route, counts, n_rows)
        ys = _experts(block_expert, slot_tok, n_used, h2, w_gu, w_down, l, slot_w, n_blocks)
        xs = _combine(pos, ys, xs, mod, 5, n_rows)

    return xs[:T_LAT].reshape(BATCH, SEQ, D)
```

```python
import functools

import numpy as np
import jax
import jax.numpy as jnp
from jax import lax
from jax.experimental import pallas as pl
from jax.experimental.pallas import tpu as pltpu

D = 2048
BATCH = 4
SEQ = 4096
DEPTH = 2
GRID_W = 64
CTX = 256
ROPE_THETA = 10000.0
EPS = 1e-6
KVH = 2
GROUP = 4
HD = 128
CONV_DIM = 1024
MLA_H = 8
Q_LORA = 512
KV_LORA = 256
NOPE = 128
ROPE = 64
MLA_V = 128
MLA_QK = NOPE + ROPE
MLA_PAD = 256
N_GROUPS = 4
EPG = 8
N_EXPERTS = N_GROUPS * EPG
TOP_K = 2
D_EXPERT = 512
MOE_BLOCK = 256

T_LAT = BATCH * SEQ
T_CTX = BATCH * CTX
T_ALL = T_LAT + T_CTX
KV_COLS = 2 * KVH * HD + KV_LORA + ROPE
Q_COLS = KVH * GROUP * HD + Q_LORA
REST_OFF = KV_COLS + Q_COLS
KVQ_PAD = 1024 + Q_COLS
REST_COLS = 3 * CONV_DIM + 3 * D
ROUTE_PAD = 128

V7X_VMEM_LIMIT = 56 * 1024 * 1024
TM = 1024
ATTN_ROWS = 2048
ATTN_KEY_CHUNK = 256
BF16_SUBLANES = 16
LOG2_E = float(np.log2(np.e))

F32 = jnp.float32
BF16 = jnp.bfloat16


def _params(*sem):
    return pltpu.CompilerParams(dimension_semantics=sem, vmem_limit_bytes=V7X_VMEM_LIMIT)


def _dot(a, b):
    return jnp.dot(a, b, preferred_element_type=F32)


def _mod_kernel(c_ref, w_ref, b_ref, o_ref):
    c = c_ref[...]
    a = (c * jax.nn.sigmoid(c)).astype(BF16)
    o_ref[...] = _dot(a, w_ref[...].astype(BF16)) + b_ref[...]


def _modulation(cin, w_mod, b_mod):
    tn = 1024
    return pl.pallas_call(
        _mod_kernel,
        out_shape=jax.ShapeDtypeStruct((DEPTH, 8, 6 * D), F32),
        grid=(DEPTH, 6 * D // tn),
        in_specs=[
            pl.BlockSpec((8, D), lambda l, j: (0, 0)),
            pl.BlockSpec((None, D, tn), lambda l, j: (l, 0, j)),
            pl.BlockSpec((None, 1, tn), lambda l, j: (l, 0, j)),
        ],
        out_specs=pl.BlockSpec((None, 8, tn), lambda l, j: (l, 0, j)),
        compiler_params=_params("parallel", "parallel"),
        name="modulation",
    )(cin, w_mod, b_mod.reshape(DEPTH, 1, 6 * D))


def _norm_mod_rows(x, g, sc, sh):
    ms = jnp.mean(x * x, axis=-1, keepdims=True)
    return (x * lax.rsqrt(ms + EPS) * g) * (1.0 + sc) + sh


def _normed_matmul_kernel(x_ref, g_ref, sc_ref, sh_ref, w_ref, o_ref, h_scr):
    rows = 128

    @pl.when(pl.program_id(1) == 0)
    def _():
        def body(r, carry):
            sl = pl.ds(pl.multiple_of(r * rows, rows), rows)
            h_scr[sl, :] = _norm_mod_rows(x_ref[sl, :], g_ref[...], sc_ref[...], sh_ref[...]).astype(BF16)
            return carry

        lax.fori_loop(0, TM // rows, body, 0)

    o_ref[...] = _dot(h_scr[...], w_ref[...]).astype(o_ref.dtype)


def _normed_matmul(xs, gain, mod, sc_chunk, sh_chunk, w, layer, n_rows, tn, name):
    n = w.shape[2]
    return pl.pallas_call(
        _normed_matmul_kernel,
        out_shape=jax.ShapeDtypeStruct((n_rows, n), BF16),
        grid=(n_rows // TM, n // tn),
        in_specs=[
            pl.BlockSpec((TM, D), lambda i, j: (i, 0)),
            pl.BlockSpec((1, D), lambda i, j: (0, 0)),
            pl.BlockSpec((None, 1, D), lambda i, j: (i // (SEQ // TM), 0, sc_chunk)),
            pl.BlockSpec((None, 1, D), lambda i, j: (i // (SEQ // TM), 0, sh_chunk)),
            pl.BlockSpec((None, D, tn), lambda i, j: (layer, 0, j)),
        ],
        out_specs=pl.BlockSpec((TM, tn), lambda i, j: (i, j)),
        scratch_shapes=[pltpu.VMEM((TM, D), BF16)],
        compiler_params=_params("parallel", "arbitrary"),
        name=name,
    )(xs, gain.reshape(1, D), mod, mod, w)


def _prep_kernel(p_ref, ra_ref, rc_ref, gq_ref, gk_ref, gql_ref, gkl_ref, gmq_ref, gmk_ref, wuq_ref, wukv_ref,
                 ka_ref, va_ref, kc_ref, vc_ref, qa_ref, qc_ref):
    cos_a = ra_ref[:, 0:HD]
    sin_a = ra_ref[:, HD:2 * HD]
    cos_c = rc_ref[:, 0:128]
    sin_up = rc_ref[:, 128:256]
    sin_dn = rc_ref[:, 256:384]
    pair_ones = (lax.broadcasted_iota(jnp.int32, (256, 256), 0) // 128
                 == lax.broadcasted_iota(jnp.int32, (256, 256), 1) // 128).astype(BF16)

    def pair_sums(a, b):
        s = _dot(jnp.concatenate([a, b], axis=1).astype(BF16), pair_ones)
        return s[:, :128], s[:, 128:]

    def inv_rms(ss, width):
        return lax.rsqrt(ss * (1.0 / width) + EPS)

    def rope_a(x):
        return x * cos_a + pltpu.roll(x, HD // 2, 1) * sin_a

    def rope_c(x):
        return x * cos_c + pltpu.roll(x, 128 - ROPE // 2, 1) * sin_up + pltpu.roll(x, ROPE // 2, 1) * sin_dn

    def slab(col):
        return p_ref[:, col:col + 128].astype(F32)

    k0, k1 = slab(0), slab(HD)
    ss0, ss1 = pair_sums(k0 * k0, k1 * k1)
    ka_ref[:, 0:HD] = rope_a(k0 * inv_rms(ss0, HD) * gk_ref[...]).astype(BF16)
    ka_ref[:, HD:2 * HD] = rope_a(k1 * inv_rms(ss1, HD) * gk_ref[...]).astype(BF16)
    ones = jnp.ones((p_ref.shape[0], 128), BF16)
    for h in range(KVH):
        va_ref[:, 2 * h * HD:(2 * h + 1) * HD] = p_ref[:, (KVH + h) * HD:(KVH + h + 1) * HD]
        va_ref[:, (2 * h + 1) * HD:(2 * h + 2) * HD] = ones

    c0, c1 = slab(512), slab(640)
    krope = slab(768)
    ss_c, ss_kr = pair_sums(c0 * c0 + c1 * c1, krope * krope)
    r_c = inv_rms(ss_c, KV_LORA)
    ckv_n = jnp.concatenate([c0 * r_c * gkl_ref[:, 0:128], c1 * r_c * gkl_ref[:, 128:256]], axis=1).astype(BF16)
    kv = _dot(ckv_n, wukv_ref[...])
    krope_rot = rope_c(krope * gmk_ref[:, 128:256])
    for h2 in range(MLA_H // 2):
        kn = [kv[:, (2 * h2 + j) * 256:(2 * h2 + j) * 256 + NOPE] for j in range(2)]
        ss = pair_sums(kn[0] * kn[0], kn[1] * kn[1])
        for j in range(2):
            h = 2 * h2 + j
            r = inv_rms(ss[j] + ss_kr, MLA_QK)
            kc_ref[:, h * MLA_PAD:h * MLA_PAD + NOPE] = (kn[j] * r * gmk_ref[:, 0:128]).astype(BF16)
            kc_ref[:, h * MLA_PAD + NOPE:(h + 1) * MLA_PAD] = (krope_rot * r).astype(BF16)
            vc_ref[:, 2 * h * MLA_V:(2 * h + 1) * MLA_V] = kv[:, h * 256 + NOPE:(h + 1) * 256].astype(BF16)
            vc_ref[:, (2 * h + 1) * MLA_V:(2 * h + 2) * MLA_V] = ones

    for h2 in range(KVH * GROUP // 2):
        q = [slab(1024 + (2 * h2 + j) * HD) for j in range(2)]
        ss = pair_sums(q[0] * q[0], q[1] * q[1])
        for j in range(2):
            h = 2 * h2 + j
            qa_ref[:, h * HD:(h + 1) * HD] = (rope_a(q[j] * inv_rms(ss[j], HD) * gq_ref[...])
                                              * (HD ** -0.5 * LOG2_E)).astype(BF16)

    cq = [slab(2048 + j * 128) for j in range(Q_LORA // 128)]
    sq = functools.reduce(lambda a, b: a + b, [c * c for c in cq])
    r_q = inv_rms(pair_sums(sq, sq)[0], Q_LORA)
    cq_n = jnp.concatenate([c * r_q * gql_ref[:, j * 128:(j + 1) * 128] for j, c in enumerate(cq)], axis=1).astype(BF16)
    qc = _dot(cq_n, wuq_ref[...])
    for h2 in range(MLA_H // 2):
        qn = [qc[:, (2 * h2 + j) * MLA_PAD:(2 * h2 + j) * MLA_PAD + NOPE] for j in range(2)]
        qr = [qc[:, (2 * h2 + j) * MLA_PAD + NOPE:(2 * h2 + j + 1) * MLA_PAD] for j in range(2)]
        ss = pair_sums(qn[0] * qn[0] + qr[0] * qr[0], qn[1] * qn[1] + qr[1] * qr[1])
        for j in range(2):
            h = 2 * h2 + j
            r = inv_rms(ss[j], MLA_QK) * (MLA_QK ** -0.5 * LOG2_E)
            qc_ref[:, h * MLA_PAD:h * MLA_PAD + NOPE] = (qn[j] * r * gmq_ref[:, 0:128]).astype(BF16)
            qc_ref[:, h * MLA_PAD + NOPE:(h + 1) * MLA_PAD] = (rope_c(qr[j] * gmq_ref[:, 128:256]) * r).astype(BF16)


def _attention_prep(pkvq, rope_a, rope_c, gq, gk, gql, gkl, gmq, gmk, wuq, wukv):
    tm = 256
    n_lat = T_LAT // tm
    per_seq = SEQ // tm

    def rope_idx(i):
        return jnp.where(i < n_lat, i % per_seq, per_seq + i - n_lat)

    def full(shape):
        return pl.BlockSpec(shape, lambda i: (0, 0))

    def rows(width):
        return pl.BlockSpec((tm, width), lambda i: (i, 0))

    out_widths = (KVH * HD, 2 * KVH * HD, MLA_H * MLA_PAD, 2 * MLA_H * MLA_V, KVH * GROUP * HD, MLA_H * MLA_PAD)
    return pl.pallas_call(
        _prep_kernel,
        out_shape=[jax.ShapeDtypeStruct((T_ALL, w), BF16) for w in out_widths],
        grid=(T_ALL // tm,),
        in_specs=[
            rows(KVQ_PAD),
            pl.BlockSpec((tm, 2 * HD), lambda i: (rope_idx(i), 0)),
            pl.BlockSpec((tm, 3 * 128), lambda i: (rope_idx(i), 0)),
            full((1, HD)), full((1, HD)), full((1, Q_LORA)), full((1, KV_LORA)),
            full((1, MLA_PAD)), full((1, MLA_PAD)),
            full((Q_LORA, MLA_H * MLA_PAD)), full((KV_LORA, MLA_H * (NOPE + MLA_V))),
        ],
        out_specs=[rows(w) for w in out_widths],
        compiler_params=_params("parallel"),
        name="attention_prep",
    )(pkvq, rope_a, rope_c, gq, gk, gql, gkl, gmq, gmk, wuq, wukv)


def _attn_kernel(q_ref, *refs, group, dk, dv, chunk):
    segments = [(refs[2 * i], refs[2 * i + 1]) for i in range((len(refs) - 1) // 2)]
    o_ref = refs[-1]
    tq = q_ref.shape[0]
    if group > 1:
        q = jnp.concatenate([q_ref[:, g * dk:(g + 1) * dk] for g in range(group)], axis=0)
    else:
        q = q_ref[...]
    m = None
    for k_ref, v_ref in segments:
        n_keys = k_ref.shape[0]
        step = min(chunk, n_keys)
        for c0 in range(0, n_keys, step):
            s = lax.dot_general(q, k_ref[c0:c0 + step, :], (((1,), (1,)), ((), ())), preferred_element_type=F32)
            c_max = jnp.max(s, axis=-1, keepdims=True)
            if m is None:
                m = c_max
                acc = _dot(jnp.exp2(s - m).astype(BF16), v_ref[c0:c0 + step, :])
            else:
                m_new = jnp.maximum(m, c_max)
                acc = jnp.exp2(m - m_new) * acc + _dot(jnp.exp2(s - m_new).astype(BF16), v_ref[c0:c0 + step, :])
                m = m_new
    o = acc[:, :dv] / acc[:, dv:]
    for g in range(group):
        o_ref[:, g * dv:(g + 1) * dv] = o[g * tq:(g + 1) * tq].astype(o_ref.dtype)


def _attention_latent(q, k, v, *, heads, group, dk, dv, tq, chunk, name):
    n_qt = SEQ // tq
    kern = functools.partial(_attn_kernel, group=group, dk=dk, dv=dv, chunk=chunk)
    return pl.pallas_call(
        kern,
        out_shape=jax.ShapeDtypeStruct((T_LAT, heads * group * dv), BF16),
        grid=(BATCH, heads, n_qt),
        in_specs=[
            pl.BlockSpec((tq, group * dk), lambda b, h, qi: (b * n_qt + qi, h)),
            pl.BlockSpec((CTX, dk), lambda b, h, qi: (T_LAT // CTX + b, h)),
            pl.BlockSpec((CTX, 2 * dv), lambda b, h, qi: (T_LAT // CTX + b, h)),
            pl.BlockSpec((SEQ, dk), lambda b, h, qi: (b, h)),
            pl.BlockSpec((SEQ, 2 * dv), lambda b, h, qi: (b, h)),
        ],
        out_specs=pl.BlockSpec((tq, group * dv), lambda b, h, qi: (b * n_qt + qi, h)),
        compiler_params=_params("parallel", "parallel", "arbitrary"),
        name=name,
    )(q, k, v, k, v)


def _attention_context(q, k, v, *, heads, group, dk, dv, name):
    kern = functools.partial(_attn_kernel, group=group, dk=dk, dv=dv, chunk=CTX)
    ctx0 = T_LAT // CTX
    return pl.pallas_call(
        kern,
        out_shape=jax.ShapeDtypeStruct((T_CTX, heads * group * dv), BF16),
        grid=(BATCH, heads),
        in_specs=[
            pl.BlockSpec((CTX, group * dk), lambda b, h: (ctx0 + b, h)),
            pl.BlockSpec((CTX, dk), lambda b, h: (ctx0 + b, h)),
            pl.BlockSpec((CTX, 2 * dv), lambda b, h: (ctx0 + b, h)),
        ],
        out_specs=pl.BlockSpec((CTX, group * dv), lambda b, h: (b, h)),
        compiler_params=_params("parallel", "parallel"),
        name=name,
    )(q, k, v)


def _attention(q, k, v, *, heads, group, dk, dv, tq, chunk, with_ctx_queries, name):
    y = _attention_latent(q, k, v, heads=heads, group=group, dk=dk, dv=dv, tq=tq, chunk=chunk, name=name)
    if not with_ctx_queries:
        return y, y[:T_CTX]
    return y, _attention_context(q, k, v, heads=heads, group=group, dk=dk, dv=dv, name=name + "_ctx")


def _conv_kernel(xb_ref, gb_ref, gc_ref, xbp_ref, gcp_ref, xbn_ref, gcn_ref, w_ref, o_ref):
    i = pl.program_id(0)
    z = gc_ref[...].astype(F32) * xb_ref[...].astype(F32)
    last = BF16_SUBLANES - 1
    z_prev = gcp_ref[last:last + 1, :].astype(F32) * xbp_ref[last:last + 1, :].astype(F32)
    z_next = gcn_ref[0:1, :].astype(F32) * xbn_ref[0:1, :].astype(F32)
    row = lax.broadcasted_iota(jnp.int32, (TM, 1), 0)
    seq_len = jnp.where(i < T_LAT // TM, SEQ, CTX)
    pos = (i * TM + row) & (seq_len - 1)
    z_dn = jnp.where(row == 0, z_prev, pltpu.roll(z, 1, 0))
    z_up = jnp.where(row == TM - 1, z_next, pltpu.roll(z, TM - 1, 0))
    z_dn = jnp.where(pos == 0, 0.0, z_dn)
    z_up = jnp.where(pos == seq_len - 1, 0.0, z_up)
    y = gb_ref[...].astype(F32) * (w_ref[0:1, :] * z_dn + w_ref[1:2, :] * z + w_ref[2:3, :] * z_up)
    o_ref[...] = y.astype(BF16)


def _conv_branch(prest, conv_w, n_rows):
    tc = 256
    n_ct = CONV_DIM // tc
    halo = TM // BF16_SUBLANES
    last_halo = n_rows // BF16_SUBLANES - 1

    def main(col0):
        return pl.BlockSpec((TM, tc), lambda i, j: (i, col0 * n_ct + j))

    def prev(col0):
        return pl.BlockSpec((BF16_SUBLANES, tc), lambda i, j: (jnp.maximum(i * halo - 1, 0), col0 * n_ct + j))

    def nxt(col0):
        return pl.BlockSpec((BF16_SUBLANES, tc), lambda i, j: (jnp.minimum((i + 1) * halo, last_halo), col0 * n_ct + j))

    return pl.pallas_call(
        _conv_kernel,
        out_shape=jax.ShapeDtypeStruct((n_rows, CONV_DIM), BF16),
        grid=(n_rows // TM, n_ct),
        in_specs=[main(0), main(1), main(2), prev(0), prev(2), nxt(0), nxt(2),
                  pl.BlockSpec((3, tc), lambda i, j: (0, j))],
        out_specs=pl.BlockSpec((TM, tc), lambda i, j: (i, j)),
        compiler_params=_params("parallel", "parallel"),
        name="conv_branch",
    )(prest, prest, prest, prest, prest, prest, prest, conv_w)


def _merge_kernel(ya_ref, yax_ref, yb_ref, yc_ref, ycx_ref, ga_ref, gb_ref, gc_ref, ba_ref, bb_ref, bc_ref,
                  wa_ref, wb_ref, wc_ref, o_ref):
    is_ctx = pl.program_id(0) == T_LAT // TM

    def branch(y, w_ref, g_ref, b_ref):
        return jax.nn.sigmoid(g_ref[...].astype(F32) + b_ref[...]) * _dot(y, w_ref[...])

    m = branch(jnp.where(is_ctx, yax_ref[...], ya_ref[...]), wa_ref, ga_ref, ba_ref)
    m = m + branch(yb_ref[...], wb_ref, gb_ref, bb_ref)
    m = m + branch(jnp.where(is_ctx, ycx_ref[...], yc_ref[...]), wc_ref, gc_ref, bc_ref)
    o_ref[...] = m.astype(BF16)


def _merge(ya, ya_ctx, yb, yc, yc_ctx, prest, b_gate, wa, wb, wc, layer, n_rows):
    tn = 512
    n_ct = D // tn
    gate_blk0 = 3 * CONV_DIM // tn
    last_lat = T_LAT // TM - 1

    def y_spec():
        return pl.BlockSpec((TM, CONV_DIM), lambda i, j: (i, 0))

    def y_lat_spec():
        return pl.BlockSpec((TM, CONV_DIM), lambda i, j: (jnp.minimum(i, last_lat), 0))

    def y_ctx_spec():
        return pl.BlockSpec((TM, CONV_DIM), lambda i, j: (0, 0))

    def gate_spec(k):
        return pl.BlockSpec((TM, tn), lambda i, j: (i, gate_blk0 + k * n_ct + j))

    def bias_spec(k):
        return pl.BlockSpec((1, tn), lambda i, j: (0, k * n_ct + j))

    def w_spec():
        return pl.BlockSpec((None, CONV_DIM, tn), lambda i, j: (layer, 0, j))

    return pl.pallas_call(
        _merge_kernel,
        out_shape=jax.ShapeDtypeStruct((n_rows, D), BF16),
        grid=(n_rows // TM, n_ct),
        in_specs=[y_lat_spec(), y_ctx_spec(), y_spec(), y_lat_spec(), y_ctx_spec(),
                  gate_spec(0), gate_spec(1), gate_spec(2),
                  bias_spec(0), bias_spec(1), bias_spec(2), w_spec(), w_spec(), w_spec()],
        out_specs=pl.BlockSpec((TM, tn), lambda i, j: (i, j)),
        compiler_params=_params("parallel", "arbitrary"),
        name="merge_branches",
    )(ya, ya_ctx, yb, yc, yc_ctx, prest, prest, prest, b_gate, b_gate, b_gate, wa, wb, wc)


def _oproj_kernel(m_ref, w_ref, x_ref, g_ref, o_ref):
    o_ref[...] = x_ref[...] + g_ref[...] * _dot(m_ref[...], w_ref[...])


def _oproj_residual(m, w_o, layer, xs, mod, gate_chunk, n_rows):
    tm = TM // 2
    return pl.pallas_call(
        _oproj_kernel,
        out_shape=jax.ShapeDtypeStruct((n_rows, D), F32),
        grid=(n_rows // tm,),
        in_specs=[
            pl.BlockSpec((tm, D), lambda i: (i, 0)),
            pl.BlockSpec((None, D, D), lambda i: (layer, 0, 0)),
            pl.BlockSpec((tm, D), lambda i: (i, 0)),
            pl.BlockSpec((None, 1, D), lambda i: (i // (SEQ // tm), 0, gate_chunk)),
        ],
        out_specs=pl.BlockSpec((tm, D), lambda i: (i, 0)),
        compiler_params=_params("parallel"),
        name="oproj_residual",
    )(m, w_o, xs, mod)


def _router_kernel(x_ref, g_ref, sc_ref, sh_ref, whi_ref, wlo_ref, b_ref, h_ref, route_ref, cnt_ref, cnt_scr):
    cols = 128

    @pl.when(pl.program_id(0) == 0)
    def _():
        cnt_scr[...] = jnp.zeros_like(cnt_scr)

    tri = (lax.broadcasted_iota(jnp.int32, (cols, cols), 0)
           <= lax.broadcasted_iota(jnp.int32, (cols, cols), 1)).astype(BF16)
    sub = lax.broadcasted_iota(jnp.int32, (EPG, cols), 0).astype(F32)
    e_row = lax.broadcasted_iota(jnp.int32, (N_EXPERTS, cols), 0).astype(F32)
    nt_dims = (((1,), (1,)), ((), ()))
    neg = -jnp.inf

    def body(r, carry):
        sl = pl.ds(pl.multiple_of(r * cols, cols), cols)
        h = _norm_mod_rows(x_ref[sl, :], g_ref[...], sc_ref[...], sh_ref[...])
        for c in range(D // 128):
            h_ref[sl, c, :] = h[:, c * 128:(c + 1) * 128]
        h_hi = h.astype(BF16)
        h_lo = (h - h_hi.astype(F32)).astype(BF16)

        def wt_dot(w_ref, act):
            return lax.dot_general(w_ref[...], act, nt_dims, preferred_element_type=F32)

        lg = wt_dot(whi_ref, h_hi) + (wt_dot(wlo_ref, h_hi) + wt_dot(whi_ref, h_lo)) + b_ref[...]

        g_logit = [lg[N_EXPERTS + g:N_EXPERTS + g + 1, :] for g in range(N_GROUPS)]
        g_max = functools.reduce(jnp.maximum, g_logit)
        g_prob = 1.0 / functools.reduce(lambda a, b: a + b, [jnp.exp(v - g_max) for v in g_logit])
        g_idx = jnp.full_like(g_max, float(N_GROUPS - 1))
        for g in range(N_GROUPS - 2, -1, -1):
            g_idx = jnp.where(g_logit[g] == g_max, float(g), g_idx)

        cand = lg[(N_GROUPS - 1) * EPG:N_GROUPS * EPG, :]
        for g in range(N_GROUPS - 2, -1, -1):
            cand = jnp.where(g_idx == float(g), lg[g * EPG:(g + 1) * EPG, :], cand)
        m0 = jnp.max(cand, axis=0, keepdims=True)
        i0 = jnp.min(jnp.where(cand == m0, sub, float(EPG)), axis=0, keepdims=True)
        cand = jnp.where(sub == i0, neg, cand)
        m1 = jnp.max(cand, axis=0, keepdims=True)
        i1 = jnp.min(jnp.where(cand == m1, sub, float(EPG)), axis=0, keepdims=True)
        t = jnp.exp(m1 - m0)
        w0 = g_prob / (1.0 + t)
        w1 = g_prob * t / (1.0 + t)
        e0 = g_idx * EPG + i0
        e1 = g_idx * EPG + i1

        hit0 = e_row == e0
        hit1 = e_row == e1
        hits = jnp.where(hit0 | hit1, 1.0, 0.0)
        prefix = _dot(hits.astype(BF16), tri)
        before = prefix - hits + cnt_scr[...]
        rank0 = jnp.sum(jnp.where(hit0, before, 0.0), axis=0, keepdims=True)
        rank1 = jnp.sum(jnp.where(hit1, before, 0.0), axis=0, keepdims=True)
        cnt_scr[...] = cnt_scr[...] + prefix[:, cols - 1:cols]
        route_ref[:, sl] = jnp.concatenate([e0, e1, w0, w1, rank0, rank1, jnp.zeros((2, cols), F32)], axis=0)
        return carry

    lax.fori_loop(0, TM // cols, body, 0, unroll=2)
    cnt_ref[...] = cnt_scr[...]


def _router(xs, gain, mod, sc_chunk, sh_chunk, w_hi, w_lo, bias, n_rows):
    return pl.pallas_call(
        _router_kernel,
        out_shape=[jax.ShapeDtypeStruct((n_rows, D // 128, 128), F32), jax.ShapeDtypeStruct((8, n_rows), F32),
                   jax.ShapeDtypeStruct((N_EXPERTS, 1), F32)],
        grid=(n_rows // TM,),
        in_specs=[
            pl.BlockSpec((TM, D), lambda i: (i, 0)),
            pl.BlockSpec((1, D), lambda i: (0, 0)),
            pl.BlockSpec((None, 1, D), lambda i: (i // (SEQ // TM), 0, sc_chunk)),
            pl.BlockSpec((None, 1, D), lambda i: (i // (SEQ // TM), 0, sh_chunk)),
            pl.BlockSpec((ROUTE_PAD, D), lambda i: (0, 0)),
            pl.BlockSpec((ROUTE_PAD, D), lambda i: (0, 0)),
            pl.BlockSpec((ROUTE_PAD, 1), lambda i: (0, 0)),
        ],
        out_specs=[pl.BlockSpec((TM, D // 128, 128), lambda i: (i, 0, 0)), pl.BlockSpec((8, TM), lambda i: (0, i)),
                   pl.BlockSpec((N_EXPERTS, 1), lambda i: (0, 0))],
        scratch_shapes=[pltpu.VMEM((N_EXPERTS, 1), F32)],
        compiler_params=_params("arbitrary"),
        name="moe_router",
    )(xs, gain.reshape(1, D), mod, mod, w_hi, w_lo, bias)


def _slab_copy(src_hbm, token, dst, dst_row, sem):
    return pltpu.make_async_copy(src_hbm.at[token], dst.at[:, dst_row, :], sem)


def _start_slab_gather(src_hbm, dst, sem, n, index_of):
    def issue(r, carry):
        _slab_copy(src_hbm, index_of(r), dst, r, sem).start()
        return carry

    lax.fori_loop(0, n, issue, 0, unroll=8)


def _wait_slab_gather(src_hbm, dst, sem, n):
    def drain(r, carry):
        _slab_copy(src_hbm, 0, dst, r, sem).wait()
        return carry

    lax.fori_loop(0, n, drain, 0, unroll=8)


EXPERT_K_SPLIT = 8
N_CHUNKS = D // 128


def _expert_kernel(be_ref, tok_ref, nused_ref, h_hbm, wgu_ref, wdn_ref, sw_ref, ys_hbm,
                   xbuf0, xbuf1, ybuf0, ybuf1, zbuf, sem_in, sem_out):
    i = pl.program_id(0)
    n_used = nused_ref[0]
    xbufs = (xbuf0, xbuf1)
    ybufs = (ybuf0, ybuf1)
    rows_per_group = MOE_BLOCK // EXPERT_K_SPLIT
    chunks_per_group = N_CHUNKS // EXPERT_K_SPLIT
    kc = D // EXPERT_K_SPLIT

    def issue(blk, s, r0, r1):
        for r in range(r0, r1):
            _slab_copy(h_hbm, tok_ref[blk * MOE_BLOCK + r], xbufs[s], r, sem_in.at[s]).start()

    def out_copy(blk, s, c, src=None):
        src = ybufs[s] if src is None else src
        return pltpu.make_async_copy(src.at[c], ys_hbm.at[pl.ds(blk * MOE_BLOCK, MOE_BLOCK), c, :], sem_out.at[s])

    @pl.when(i == 0)
    def _():
        issue(0, 0, 0, MOE_BLOCK)
        zbuf[...] = jnp.zeros_like(zbuf)

    for s in range(2):
        mine = (i & 1) == s

        @pl.when(mine & (i <= n_used))
        def _():
            _wait_slab_gather(h_hbm, xbufs[s], sem_in.at[s], MOE_BLOCK)

        @pl.when(mine & (i >= 2) & (i < n_used + 2))
        def _():
            for c in range(N_CHUNKS):
                out_copy(0, s, c).wait()

        @pl.when(mine & (i < n_used))
        def _():
            gu = None
            for kk in range(EXPERT_K_SPLIT):
                issue(i + 1, 1 - s, kk * rows_per_group, (kk + 1) * rows_per_group)
                xk = jnp.concatenate([xbufs[s][kk * chunks_per_group + c] for c in range(chunks_per_group)], axis=1)
                part = _dot(xk.astype(BF16), wgu_ref[kk * kc:(kk + 1) * kc, :].astype(BF16))
                gu = part if gu is None else gu + part
            gate = gu[:, :D_EXPERT]
            act = gate * jax.nn.sigmoid(gate) * gu[:, D_EXPERT:]
            y = _dot(act.astype(BF16), wdn_ref[...].astype(BF16)) * sw_ref[...]
            for c in range(N_CHUNKS):
                ybufs[s][c] = y[:, c * 128:(c + 1) * 128]
            for c in range(N_CHUNKS):
                out_copy(i, s, c).start()

        @pl.when(mine & (i >= n_used))
        def _():
            for c in range(N_CHUNKS):
                out_copy(i, s, c, zbuf).start()
            for c in range(N_CHUNKS):
                out_copy(i, s, c, zbuf).wait()


def _experts(block_expert, slot_tok, n_used, h2, w_gu, w_down, layer, slot_w, n_blocks):
    grid_spec = pltpu.PrefetchScalarGridSpec(
        num_scalar_prefetch=3,
        grid=(n_blocks,),
        in_specs=[
            pl.BlockSpec(memory_space=pl.ANY),
            pl.BlockSpec((None, None, D, 2 * D_EXPERT), lambda i, be, tok, nu: (layer, be[i], 0, 0)),
            pl.BlockSpec((None, None, D_EXPERT, D), lambda i, be, tok, nu: (layer, be[i], 0, 0)),
            pl.BlockSpec((MOE_BLOCK, 1), lambda i, be, tok, nu: (i, 0)),
        ],
        out_specs=pl.BlockSpec(memory_space=pl.ANY),
        scratch_shapes=[pltpu.VMEM((N_CHUNKS, MOE_BLOCK, 128), F32) for _ in range(5)]
        + [pltpu.SemaphoreType.DMA((2,)), pltpu.SemaphoreType.DMA((2,))],
    )
    return pl.pallas_call(
        _expert_kernel,
        out_shape=jax.ShapeDtypeStruct((n_blocks * MOE_BLOCK, N_CHUNKS, 128), F32),
        grid_spec=grid_spec,
        compiler_params=_params("arbitrary"),
        name="moe_experts",
    )(block_expert, slot_tok, n_used, h2, w_gu, w_down, slot_w)


def _combine_kernel(pos_ref, ys_hbm, x_ref, g_ref, o_ref, buf, sem):
    i = pl.program_id(0)
    n_tiles = pl.num_programs(0)
    rows = x_ref.shape[0]
    slot = i & 1

    def start(tile, s):
        _start_slab_gather(ys_hbm, buf.at[s], sem.at[s], TOP_K * rows,
                           lambda r: pos_ref[(tile * rows + (r & (rows - 1))) * TOP_K + (r >> int(np.log2(rows)))])

    @pl.when(i == 0)
    def _():
        start(0, 0)

    _wait_slab_gather(ys_hbm, buf.at[slot], sem.at[slot], TOP_K * rows)

    @pl.when(i + 1 < n_tiles)
    def _():
        start(i + 1, 1 - slot)

    for c in range(N_CHUNKS):
        lanes = slice(c * 128, (c + 1) * 128)
        o_ref[:, lanes] = x_ref[:, lanes] + g_ref[:, lanes] * (buf[slot, c, 0:rows, :] + buf[slot, c, rows:2 * rows, :])


def _combine(pos, ys, xs, mod, gate_chunk, n_rows):
    rows = 256
    grid_spec = pltpu.PrefetchScalarGridSpec(
        num_scalar_prefetch=1,
        grid=(n_rows // rows,),
        in_specs=[
            pl.BlockSpec(memory_space=pl.ANY),
            pl.BlockSpec((rows, D), lambda i, pos: (i, 0)),
            pl.BlockSpec((None, 1, D), lambda i, pos: (i // (SEQ // rows), 0, gate_chunk)),
        ],
        out_specs=pl.BlockSpec((rows, D), lambda i, pos: (i, 0)),
        scratch_shapes=[pltpu.VMEM((2, N_CHUNKS, TOP_K * rows, 128), F32), pltpu.SemaphoreType.DMA((2,))],
    )
    return pl.pallas_call(
        _combine_kernel,
        out_shape=jax.ShapeDtypeStruct((n_rows, D), F32),
        grid_spec=grid_spec,
        compiler_params=_params("arbitrary"),
        name="moe_combine",
    )(pos, ys, xs, mod)


def _slot_layout(route, counts, n_tok):
    n_assign = n_tok * TOP_K
    expert = route[0:TOP_K, :].T.astype(jnp.int32)
    weight = route[TOP_K:2 * TOP_K, :].T
    rank = route[2 * TOP_K:3 * TOP_K, :].T.astype(jnp.int32)
    sizes = counts[:, 0].astype(jnp.int32)
    padded = (sizes + MOE_BLOCK - 1) // MOE_BLOCK * MOE_BLOCK
    pad_end = jnp.cumsum(padded)
    pad_start = pad_end - padded
    pos = (pad_start[expert] + rank).reshape(n_assign)
    n_blocks = (n_assign + MOE_BLOCK - 1) // MOE_BLOCK + N_EXPERTS + 2
    n_slots = n_blocks * MOE_BLOCK
    slot_assign = jnp.full((n_slots,), n_assign, jnp.int32).at[pos].set(
        jnp.arange(n_assign, dtype=jnp.int32), unique_indices=True)
    is_real = slot_assign < n_assign
    slot_tok = jnp.where(is_real, slot_assign // TOP_K, 0)
    weight_ext = jnp.concatenate([weight.reshape(n_assign), jnp.zeros((1,), F32)])
    slot_w = weight_ext[slot_assign]
    block_start = jnp.arange(n_blocks, dtype=jnp.int32) * MOE_BLOCK
    block_expert = jnp.minimum(jnp.sum((pad_end[None, :] <= block_start[:, None]).astype(jnp.int32), axis=1),
                               N_EXPERTS - 1)
    n_used = (pad_end[-1] // MOE_BLOCK).reshape(1)
    return block_expert, slot_tok, slot_w.reshape(n_slots, 1), pos, n_used, n_blocks


def _rope_tables():
    s = jnp.arange(SEQ)
    rows = (s // GRID_W).astype(F32)
    cols = (s % GRID_W).astype(F32)

    def angles(rot_dim):
        n = rot_dim // 4
        inv = ROPE_THETA ** (-jnp.arange(n, dtype=F32) / n)
        return jnp.concatenate([rows[:, None] * inv, cols[:, None] * inv], axis=-1)

    ang_a = angles(HD)
    cos_a, sin_a = jnp.cos(ang_a), jnp.sin(ang_a)
    tab_a = jnp.concatenate([cos_a, cos_a, -sin_a, sin_a], axis=-1)
    ident_a = jnp.concatenate([jnp.ones((T_CTX, HD), F32), jnp.zeros((T_CTX, HD), F32)], axis=-1)
    ang_c = angles(ROPE)
    cos_c, sin_c = jnp.cos(ang_c), jnp.sin(ang_c)
    half = ROPE // 2
    z_half = jnp.zeros((SEQ, half), F32)
    z_pad = jnp.zeros((SEQ, 128 - ROPE), F32)
    tab_c = jnp.concatenate([cos_c, cos_c, z_pad, -sin_c, z_half, z_pad, z_half, sin_c, z_pad], axis=-1)
    ident_c = jnp.concatenate([jnp.ones((T_CTX, ROPE), F32), jnp.zeros((T_CTX, 3 * 128 - ROPE), F32)], axis=-1)
    return jnp.concatenate([tab_a, ident_a], axis=0), jnp.concatenate([tab_c, ident_c], axis=0)


def _pad_head_cols(w, lead):
    w = w.reshape(lead, MLA_H, MLA_QK)
    return jnp.pad(w, ((0, 0), (0, 0), (0, MLA_PAD - MLA_QK))).reshape(lead, MLA_H * MLA_PAD)


def kernel(x, c, ctx, c_ctx, w_mod, b_mod, norm1, norm2, w_in, gqa_q_norm, gqa_k_norm, mla_q_lora_norm, w_uq,
           mla_kv_lora_norm, w_ukv, mla_q_norm, mla_k_norm, conv_w, w_out_a, w_out_b, w_out_c, b_gate, w_o,
           w_group, b_group, w_router, b_router, w_gu, w_down):
    xs = jnp.concatenate([x.reshape(T_LAT, D), ctx.reshape(T_CTX, D)], axis=0)
    cin = jnp.concatenate([c, c_ctx[None, :], jnp.zeros((3, D), F32)], axis=0)
    mod_all = _modulation(cin, w_mod, b_mod)
    rope_a, rope_c = _rope_tables()

    w_in_b = w_in.astype(BF16)
    w_kvq = jnp.concatenate([w_in_b[:, :, :KV_COLS], jnp.zeros((DEPTH, D, 1024 - KV_COLS), BF16),
                             w_in_b[:, :, KV_COLS:REST_OFF]], axis=2)
    w_rest = w_in_b[:, :, REST_OFF:]
    w_a, w_b, w_c, w_o_b = (w.astype(BF16) for w in (w_out_a, w_out_b, w_out_c, w_o))

    for l in range(DEPTH):
        last = l == DEPTH - 1
        n_rows = T_LAT if last else T_ALL
        mod = mod_all[l].reshape(8, 1, 6 * D)

        pkvq = _normed_matmul(xs, norm1[l], mod, 1, 0, w_kvq, l, T_ALL, KVQ_PAD // 2, "in_proj_kvq")
        prest = _normed_matmul(xs, norm1[l], mod, 1, 0, w_rest, l, n_rows, 1536, "in_proj_rest")

        ka, va, kc, vc, qa, qc = _attention_prep(
            pkvq, rope_a, rope_c,
            gqa_q_norm[l].reshape(1, HD), gqa_k_norm[l].reshape(1, HD),
            mla_q_lora_norm[l].reshape(1, Q_LORA), mla_kv_lora_norm[l].reshape(1, KV_LORA),
            jnp.pad(mla_q_norm[l], (0, MLA_PAD - MLA_QK)).reshape(1, MLA_PAD),
            jnp.pad(mla_k_norm[l], (0, MLA_PAD - MLA_QK)).reshape(1, MLA_PAD),
            _pad_head_cols(w_uq[l], Q_LORA).astype(BF16), w_ukv[l].astype(BF16))

        y_a, y_a_ctx = _attention(qa, ka, va, heads=KVH, group=GROUP, dk=HD, dv=HD, tq=ATTN_ROWS // GROUP,
                                  chunk=ATTN_KEY_CHUNK,
                         with_ctx_queries=not last, name="gqa_attention")
        y_c, y_c_ctx = _attention(qc, kc, vc, heads=MLA_H, group=1, dk=MLA_PAD, dv=MLA_V, tq=ATTN_ROWS,
                                  chunk=ATTN_KEY_CHUNK,
                         with_ctx_queries=not last, name="mla_attention")
        y_b = _conv_branch(prest, conv_w[l], n_rows)
        m = _merge(y_a, y_a_ctx, y_b, y_c, y_c_ctx, prest, b_gate[l].reshape(1, 3 * D), w_a, w_b, w_c, l, n_rows)
        xs = _oproj_residual(m, w_o_b, l, xs, mod, 2, n_rows)

        w_r = jnp.concatenate([w_router[l], w_group[l], jnp.zeros((D, ROUTE_PAD - N_GROUPS - N_EXPERTS), F32)], axis=1).T
        w_r_hi = w_r.astype(BF16)
        w_r_lo = (w_r - w_r_hi.astype(F32)).astype(BF16)
        b_r = jnp.concatenate([b_router[l], b_group[l], jnp.zeros((ROUTE_PAD - N_GROUPS - N_EXPERTS,), F32)])
        h2, route, counts = _router(xs, norm2[l], mod, 4, 3, w_r_hi, w_r_lo, b_r.reshape(ROUTE_PAD, 1), n_rows)
        block_expert, slot_tok, slot_w, pos, n_used, n_blocks = _slot_layout(route, counts, n_rows)
        ys = _experts(block_expert, slot_tok, n_used, h2, w_gu, w_down, l, slot_w, n_blocks)
        xs = _combine(pos, ys, xs, mod, 5, n_rows)

    return xs[:T_LAT].reshape(BATCH, SEQ, D)
```

```python
import functools

import numpy as np
import jax
import jax.numpy as jnp
from jax import lax
from jax.experimental import pallas as pl
from jax.experimental.pallas import tpu as pltpu

D = 2048
BATCH = 4
SEQ = 4096
DEPTH = 2
GRID_W = 64
CTX = 256
ROPE_THETA = 10000.0
EPS = 1e-6
KVH = 2
GROUP = 4
HD = 128
CONV_DIM = 1024
MLA_H = 8
Q_LORA = 512
KV_LORA = 256
NOPE = 128
ROPE = 64
MLA_V = 128
MLA_QK = NOPE + ROPE
MLA_PAD = 256
N_GROUPS = 4
EPG = 8
N_EXPERTS = N_GROUPS * EPG
TOP_K = 2
D_EXPERT = 512
MOE_BLOCK = 256

T_LAT = BATCH * SEQ
T_CTX = BATCH * CTX
T_ALL = T_LAT + T_CTX
KV_COLS = 2 * KVH * HD + KV_LORA + ROPE
Q_COLS = KVH * GROUP * HD + Q_LORA
REST_OFF = KV_COLS + Q_COLS
KVQ_PAD = 1024 + Q_COLS
REST_COLS = 3 * CONV_DIM + 3 * D
ROUTE_PAD = 128

V7X_VMEM_LIMIT = 56 * 1024 * 1024
TM = 1024
ATTN_ROWS = 2048
ATTN_KEY_CHUNK = 256
BF16_SUBLANES = 16
LOG2_E = float(np.log2(np.e))

F32 = jnp.float32
BF16 = jnp.bfloat16


def _params(*sem):
    return pltpu.CompilerParams(dimension_semantics=sem, vmem_limit_bytes=V7X_VMEM_LIMIT)


def _dot(a, b):
    return jnp.dot(a, b, preferred_element_type=F32)


def _mod_kernel(c_ref, w_ref, b_ref, o_ref):
    c = c_ref[...]
    a = (c * jax.nn.sigmoid(c)).astype(BF16)
    o_ref[...] = _dot(a, w_ref[...].astype(BF16)) + b_ref[...]


def _modulation(cin, w_mod, b_mod):
    tn = 1024
    return pl.pallas_call(
        _mod_kernel,
        out_shape=jax.ShapeDtypeStruct((DEPTH, 8, 6 * D), F32),
        grid=(DEPTH, 6 * D // tn),
        in_specs=[
            pl.BlockSpec((8, D), lambda l, j: (0, 0)),
            pl.BlockSpec((None, D, tn), lambda l, j: (l, 0, j)),
            pl.BlockSpec((None, 1, tn), lambda l, j: (l, 0, j)),
        ],
        out_specs=pl.BlockSpec((None, 8, tn), lambda l, j: (l, 0, j)),
        compiler_params=_params("parallel", "parallel"),
        name="modulation",
    )(cin, w_mod, b_mod.reshape(DEPTH, 1, 6 * D))


def _norm_mod_rows(x, g, sc, sh):
    ms = jnp.mean(x * x, axis=-1, keepdims=True)
    return (x * lax.rsqrt(ms + EPS) * g) * (1.0 + sc) + sh


def _normed_matmul_kernel(x_ref, g_ref, sc_ref, sh_ref, w_ref, o_ref, h_scr):
    rows = 128

    @pl.when(pl.program_id(1) == 0)
    def _():
        def body(r, carry):
            sl = pl.ds(pl.multiple_of(r * rows, rows), rows)
            h_scr[sl, :] = _norm_mod_rows(x_ref[sl, :], g_ref[...], sc_ref[...], sh_ref[...]).astype(BF16)
            return carry

        lax.fori_loop(0, TM // rows, body, 0)

    o_ref[...] = _dot(h_scr[...], w_ref[...]).astype(o_ref.dtype)


def _normed_matmul(xs, gain, mod, sc_chunk, sh_chunk, w, layer, n_rows, tn, name):
    n = w.shape[2]
    return pl.pallas_call(
        _normed_matmul_kernel,
        out_shape=jax.ShapeDtypeStruct((n_rows, n), BF16),
        grid=(n_rows // TM, n // tn),
        in_specs=[
            pl.BlockSpec((TM, D), lambda i, j: (i, 0)),
            pl.BlockSpec((1, D), lambda i, j: (0, 0)),
            pl.BlockSpec((None, 1, D), lambda i, j: (i // (SEQ // TM), 0, sc_chunk)),
            pl.BlockSpec((None, 1, D), lambda i, j: (i // (SEQ // TM), 0, sh_chunk)),
            pl.BlockSpec((None, D, tn), lambda i, j: (layer, 0, j)),
        ],
        out_specs=pl.BlockSpec((TM, tn), lambda i, j: (i, j)),
        scratch_shapes=[pltpu.VMEM((TM, D), BF16)],
        compiler_params=_params("parallel", "arbitrary"),
        name=name,
    )(xs, gain.reshape(1, D), mod, mod, w)


def _prep_kernel(p_ref, ra_ref, rc_ref, gq_ref, gk_ref, gql_ref, gkl_ref, gmq_ref, gmk_ref, wuq_ref, wukv_ref,
                 ka_ref, va_ref, kc_ref, vc_ref, qa_ref, qc_ref):
    cos_a = ra_ref[:, 0:HD]
    sin_a = ra_ref[:, HD:2 * HD]
    cos_c = rc_ref[:, 0:128]
    sin_up = rc_ref[:, 128:256]
    sin_dn = rc_ref[:, 256:384]
    pair_ones = (lax.broadcasted_iota(jnp.int32, (256, 256), 0) // 128
                 == lax.broadcasted_iota(jnp.int32, (256, 256), 1) // 128).astype(BF16)

    def pair_sums(a, b):
        s = _dot(jnp.concatenate([a, b], axis=1).astype(BF16), pair_ones)
        return s[:, :128], s[:, 128:]

    def inv_rms(ss, width):
        return lax.rsqrt(ss * (1.0 / width) + EPS)

    def rope_a(x):
        return x * cos_a + pltpu.roll(x, HD // 2, 1) * sin_a

    def rope_c(x):
        return x * cos_c + pltpu.roll(x, 128 - ROPE // 2, 1) * sin_up + pltpu.roll(x, ROPE // 2, 1) * sin_dn

    def slab(col):
        return p_ref[:, col:col + 128].astype(F32)

    k0, k1 = slab(0), slab(HD)
    ss0, ss1 = pair_sums(k0 * k0, k1 * k1)
    ka_ref[:, 0:HD] = rope_a(k0 * inv_rms(ss0, HD) * gk_ref[...]).astype(BF16)
    ka_ref[:, HD:2 * HD] = rope_a(k1 * inv_rms(ss1, HD) * gk_ref[...]).astype(BF16)
    ones = jnp.ones((p_ref.shape[0], 128), BF16)
    for h in range(KVH):
        va_ref[:, 2 * h * HD:(2 * h + 1) * HD] = p_ref[:, (KVH + h) * HD:(KVH + h + 1) * HD]
        va_ref[:, (2 * h + 1) * HD:(2 * h + 2) * HD] = ones

    c0, c1 = slab(512), slab(640)
    krope = slab(768)
    ss_c, ss_kr = pair_sums(c0 * c0 + c1 * c1, krope * krope)
    r_c = inv_rms(ss_c, KV_LORA)
    ckv_n = jnp.concatenate([c0 * r_c * gkl_ref[:, 0:128], c1 * r_c * gkl_ref[:, 128:256]], axis=1).astype(BF16)
    kv = _dot(ckv_n, wukv_ref[...])
    krope_rot = rope_c(krope * gmk_ref[:, 128:256])
    for h2 in range(MLA_H // 2):
        kn = [kv[:, (2 * h2 + j) * 256:(2 * h2 + j) * 256 + NOPE] for j in range(2)]
        ss = pair_sums(kn[0] * kn[0], kn[1] * kn[1])
        for j in range(2):
            h = 2 * h2 + j
            r = inv_rms(ss[j] + ss_kr, MLA_QK)
            kc_ref[:, h * MLA_PAD:h * MLA_PAD + NOPE] = (kn[j] * r * gmk_ref[:, 0:128]).astype(BF16)
            kc_ref[:, h * MLA_PAD + NOPE:(h + 1) * MLA_PAD] = (krope_rot * r).astype(BF16)
            vc_ref[:, 2 * h * MLA_V:(2 * h + 1) * MLA_V] = kv[:, h * 256 + NOPE:(h + 1) * 256].astype(BF16)
            vc_ref[:, (2 * h + 1) * MLA_V:(2 * h + 2) * MLA_V] = ones

    for h2 in range(KVH * GROUP // 2):
        q = [slab(1024 + (2 * h2 + j) * HD) for j in range(2)]
        ss = pair_sums(q[0] * q[0], q[1] * q[1])
        for j in range(2):
            h = 2 * h2 + j
            qa_ref[:, h * HD:(h + 1) * HD] = (rope_a(q[j] * inv_rms(ss[j], HD) * gq_ref[...])
                                              * (HD ** -0.5 * LOG2_E)).astype(BF16)

    cq = [slab(2048 + j * 128) for j in range(Q_LORA // 128)]
    sq = functools.reduce(lambda a, b: a + b, [c * c for c in cq])
    r_q = inv_rms(pair_sums(sq, sq)[0], Q_LORA)
    cq_n = jnp.concatenate([c * r_q * gql_ref[:, j * 128:(j + 1) * 128] for j, c in enumerate(cq)], axis=1).astype(BF16)
    qc = _dot(cq_n, wuq_ref[...])
    for h2 in range(MLA_H // 2):
        qn = [qc[:, (2 * h2 + j) * MLA_PAD:(2 * h2 + j) * MLA_PAD + NOPE] for j in range(2)]
        qr = [qc[:, (2 * h2 + j) * MLA_PAD + NOPE:(2 * h2 + j + 1) * MLA_PAD] for j in range(2)]
        ss = pair_sums(qn[0] * qn[0] + qr[0] * qr[0], qn[1] * qn[1] + qr[1] * qr[1])
        for j in range(2):
            h = 2 * h2 + j
            r = inv_rms(ss[j], MLA_QK) * (MLA_QK ** -0.5 * LOG2_E)
            qc_ref[:, h * MLA_PAD:h * MLA_PAD + NOPE] = (qn[j] * r * gmq_ref[:, 0:128]).astype(BF16)
            qc_ref[:, h * MLA_PAD + NOPE:(h + 1) * MLA_PAD] = (rope_c(qr[j] * gmq_ref[:, 128:256]) * r).astype(BF16)


def _attention_prep(pkvq, rope_a, rope_c, gq, gk, gql, gkl, gmq, gmk, wuq, wukv):
    tm = 256
    n_lat = T_LAT // tm
    per_seq = SEQ // tm

    def rope_idx(i):
        return jnp.where(i < n_lat, i % per_seq, per_seq + i - n_lat)

    def full(shape):
        return pl.BlockSpec(shape, lambda i: (0, 0))

    def rows(width):
        return pl.BlockSpec((tm, width), lambda i: (i, 0))

    out_widths = (KVH * HD, 2 * KVH * HD, MLA_H * MLA_PAD, 2 * MLA_H * MLA_V, KVH * GROUP * HD, MLA_H * MLA_PAD)
    return pl.pallas_call(
        _prep_kernel,
        out_shape=[jax.ShapeDtypeStruct((T_ALL, w), BF16) for w in out_widths],
        grid=(T_ALL // tm,),
        in_specs=[
            rows(KVQ_PAD),
            pl.BlockSpec((tm, 2 * HD), lambda i: (rope_idx(i), 0)),
            pl.BlockSpec((tm, 3 * 128), lambda i: (rope_idx(i), 0)),
            full((1, HD)), full((1, HD)), full((1, Q_LORA)), full((1, KV_LORA)),
            full((1, MLA_PAD)), full((1, MLA_PAD)),
            full((Q_LORA, MLA_H * MLA_PAD)), full((KV_LORA, MLA_H * (NOPE + MLA_V))),
        ],
        out_specs=[rows(w) for w in out_widths],
        compiler_params=_params("parallel"),
        name="attention_prep",
    )(pkvq, rope_a, rope_c, gq, gk, gql, gkl, gmq, gmk, wuq, wukv)


def _attn_kernel(q_ref, *refs, group, dk, dv, chunk):
    segments = [(refs[2 * i], refs[2 * i + 1]) for i in range((len(refs) - 1) // 2)]
    o_ref = refs[-1]
    tq = q_ref.shape[0]
    if group > 1:
        q = jnp.concatenate([q_ref[:, g * dk:(g + 1) * dk] for g in range(group)], axis=0)
    else:
        q = q_ref[...]
    m = None
    for k_ref, v_ref in segments:
        n_keys = k_ref.shape[0]
        step = min(chunk, n_keys)
        for c0 in range(0, n_keys, step):
            s = lax.dot_general(q, k_ref[c0:c0 + step, :], (((1,), (1,)), ((), ())), preferred_element_type=F32)
            c_max = jnp.max(s, axis=-1, keepdims=True)
            if m is None:
                m = c_max
                acc = _dot(jnp.exp2(s - m).astype(BF16), v_ref[c0:c0 + step, :])
            else:
                m_new = jnp.maximum(m, c_max)
                acc = jnp.exp2(m - m_new) * acc + _dot(jnp.exp2(s - m_new).astype(BF16), v_ref[c0:c0 + step, :])
                m = m_new
    o = acc[:, :dv] / acc[:, dv:]
    for g in range(group):
        o_ref[:, g * dv:(g + 1) * dv] = o[g * tq:(g + 1) * tq].astype(o_ref.dtype)


def _attention_latent(q, k, v, *, heads, group, dk, dv, tq, chunk, name):
    n_qt = SEQ // tq
    kern = functools.partial(_attn_kernel, group=group, dk=dk, dv=dv, chunk=chunk)
    return pl.pallas_call(
        kern,
        out_shape=jax.ShapeDtypeStruct((T_LAT, heads * group * dv), BF16),
        grid=(BATCH, heads, n_qt),
        in_specs=[
            pl.BlockSpec((tq, group * dk), lambda b, h, qi: (b * n_qt + qi, h)),
            pl.BlockSpec((CTX, dk), lambda b, h, qi: (T_LAT // CTX + b, h)),
            pl.BlockSpec((CTX, 2 * dv), lambda b, h, qi: (T_LAT // CTX + b, h)),
            pl.BlockSpec((SEQ, dk), lambda b, h, qi: (b, h)),
            pl.BlockSpec((SEQ, 2 * dv), lambda b, h, qi: (b, h)),
        ],
        out_specs=pl.BlockSpec((tq, group * dv), lambda b, h, qi: (b * n_qt + qi, h)),
        compiler_params=_params("parallel", "parallel", "arbitrary"),
        name=name,
    )(q, k, v, k, v)


def _attention_context(q, k, v, *, heads, group, dk, dv, name):
    kern = functools.partial(_attn_kernel, group=group, dk=dk, dv=dv, chunk=CTX)
    ctx0 = T_LAT // CTX
    return pl.pallas_call(
        kern,
        out_shape=jax.ShapeDtypeStruct((T_CTX, heads * group * dv), BF16),
        grid=(BATCH, heads),
        in_specs=[
            pl.BlockSpec((CTX, group * dk), lambda b, h: (ctx0 + b, h)),
            pl.BlockSpec((CTX, dk), lambda b, h: (ctx0 + b, h)),
            pl.BlockSpec((CTX, 2 * dv), lambda b, h: (ctx0 + b, h)),
        ],
        out_specs=pl.BlockSpec((CTX, group * dv), lambda b, h: (b, h)),
        compiler_params=_params("parallel", "parallel"),
        name=name,
    )(q, k, v)


def _attention(q, k, v, *, heads, group, dk, dv, tq, chunk, with_ctx_queries, name):
    y = _attention_latent(q, k, v, heads=heads, group=group, dk=dk, dv=dv, tq=tq, chunk=chunk, name=name)
    if not with_ctx_queries:
        return y, y[:T_CTX]
    return y, _attention_context(q, k, v, heads=heads, group=group, dk=dk, dv=dv, name=name + "_ctx")


def _conv_kernel(xb_ref, gb_ref, gc_ref, xbp_ref, gcp_ref, xbn_ref, gcn_ref, w_ref, o_ref):
    i = pl.program_id(0)
    z = gc_ref[...].astype(F32) * xb_ref[...].astype(F32)
    last = BF16_SUBLANES - 1
    z_prev = gcp_ref[last:last + 1, :].astype(F32) * xbp_ref[last:last + 1, :].astype(F32)
    z_next = gcn_ref[0:1, :].astype(F32) * xbn_ref[0:1, :].astype(F32)
    row = lax.broadcasted_iota(jnp.int32, (TM, 1), 0)
    seq_len = jnp.where(i < T_LAT // TM, SEQ, CTX)
    pos = (i * TM + row) & (seq_len - 1)
    z_dn = jnp.where(row == 0, z_prev, pltpu.roll(z, 1, 0))
    z_up = jnp.where(row == TM - 1, z_next, pltpu.roll(z, TM - 1, 0))
    z_dn = jnp.where(pos == 0, 0.0, z_dn)
    z_up = jnp.where(pos == seq_len - 1, 0.0, z_up)
    y = gb_ref[...].astype(F32) * (w_ref[0:1, :] * z_dn + w_ref[1:2, :] * z + w_ref[2:3, :] * z_up)
    o_ref[...] = y.astype(BF16)


def _conv_branch(prest, conv_w, n_rows):
    tc = 256
    n_ct = CONV_DIM // tc
    halo = TM // BF16_SUBLANES
    last_halo = n_rows // BF16_SUBLANES - 1

    def main(col0):
        return pl.BlockSpec((TM, tc), lambda i, j: (i, col0 * n_ct + j))

    def prev(col0):
        return pl.BlockSpec((BF16_SUBLANES, tc), lambda i, j: (jnp.maximum(i * halo - 1, 0), col0 * n_ct + j))

    def nxt(col0):
        return pl.BlockSpec((BF16_SUBLANES, tc), lambda i, j: (jnp.minimum((i + 1) * halo, last_halo), col0 * n_ct + j))

    return pl.pallas_call(
        _conv_kernel,
        out_shape=jax.ShapeDtypeStruct((n_rows, CONV_DIM), BF16),
        grid=(n_rows // TM, n_ct),
        in_specs=[main(0), main(1), main(2), prev(0), prev(2), nxt(0), nxt(2),
                  pl.BlockSpec((3, tc), lambda i, j: (0, j))],
        out_specs=pl.BlockSpec((TM, tc), lambda i, j: (i, j)),
        compiler_params=_params("parallel", "parallel"),
        name="conv_branch",
    )(prest, prest, prest, prest, prest, prest, prest, conv_w)


def _merge_kernel(ya_ref, yax_ref, yb_ref, yc_ref, ycx_ref, ga_ref, gb_ref, gc_ref, ba_ref, bb_ref, bc_ref,
                  wa_ref, wb_ref, wc_ref, o_ref):
    is_ctx = pl.program_id(0) == T_LAT // TM

    def branch(y, w_ref, g_ref, b_ref):
        return jax.nn.sigmoid(g_ref[...].astype(F32) + b_ref[...]) * _dot(y, w_ref[...])

    m = branch(jnp.where(is_ctx, yax_ref[...], ya_ref[...]), wa_ref, ga_ref, ba_ref)
    m = m + branch(yb_ref[...], wb_ref, gb_ref, bb_ref)
    m = m + branch(jnp.where(is_ctx, ycx_ref[...], yc_ref[...]), wc_ref, gc_ref, bc_ref)
    o_ref[...] = m.astype(BF16)


def _merge(ya, ya_ctx, yb, yc, yc_ctx, prest, b_gate, wa, wb, wc, layer, n_rows):
    tn = 512
    n_ct = D // tn
    gate_blk0 = 3 * CONV_DIM // tn
    last_lat = T_LAT // TM - 1

    def y_spec():
        return pl.BlockSpec((TM, CONV_DIM), lambda i, j: (i, 0))

    def y_lat_spec():
        return pl.BlockSpec((TM, CONV_DIM), lambda i, j: (jnp.minimum(i, last_lat), 0))

    def y_ctx_spec():
        return pl.BlockSpec((TM, CONV_DIM), lambda i, j: (0, 0))

    def gate_spec(k):
        return pl.BlockSpec((TM, tn), lambda i, j: (i, gate_blk0 + k * n_ct + j))

    def bias_spec(k):
        return pl.BlockSpec((1, tn), lambda i, j: (0, k * n_ct + j))

    def w_spec():
        return pl.BlockSpec((None, CONV_DIM, tn), lambda i, j: (layer, 0, j))

    return pl.pallas_call(
        _merge_kernel,
        out_shape=jax.ShapeDtypeStruct((n_rows, D), BF16),
        grid=(n_rows // TM, n_ct),
        in_specs=[y_lat_spec(), y_ctx_spec(), y_spec(), y_lat_spec(), y_ctx_spec(),
                  gate_spec(0), gate_spec(1), gate_spec(2),
                  bias_spec(0), bias_spec(1), bias_spec(2), w_spec(), w_spec(), w_spec()],
        out_specs=pl.BlockSpec((TM, tn), lambda i, j: (i, j)),
        compiler_params=_params("parallel", "arbitrary"),
        name="merge_branches",
    )(ya, ya_ctx, yb, yc, yc_ctx, prest, prest, prest, b_gate, b_gate, b_gate, wa, wb, wc)


def _oproj_kernel(m_ref, w_ref, x_ref, g_ref, o_ref):
    o_ref[...] = x_ref[...] + g_ref[...] * _dot(m_ref[...], w_ref[...])


def _oproj_residual(m, w_o, layer, xs, mod, gate_chunk, n_rows):
    tm = TM // 2
    return pl.pallas_call(
        _oproj_kernel,
        out_shape=jax.ShapeDtypeStruct((n_rows, D), F32),
        grid=(n_rows // tm,),
        in_specs=[
            pl.BlockSpec((tm, D), lambda i: (i, 0)),
            pl.BlockSpec((None, D, D), lambda i: (layer, 0, 0)),
            pl.BlockSpec((tm, D), lambda i: (i, 0)),
            pl.BlockSpec((None, 1, D), lambda i: (i // (SEQ // tm), 0, gate_chunk)),
        ],
        out_specs=pl.BlockSpec((tm, D), lambda i: (i, 0)),
        compiler_params=_params("parallel"),
        name="oproj_residual",
    )(m, w_o, xs, mod)


def _router_kernel(x_ref, g_ref, sc_ref, sh_ref, whi_ref, wlo_ref, b_ref, h_ref, route_ref, cnt_scr):
    cols = 128

    @pl.when(pl.program_id(0) == 0)
    def _():
        cnt_scr[...] = jnp.zeros_like(cnt_scr)

    tri = (lax.broadcasted_iota(jnp.int32, (cols, cols), 0)
           <= lax.broadcasted_iota(jnp.int32, (cols, cols), 1)).astype(BF16)
    sub = lax.broadcasted_iota(jnp.int32, (EPG, cols), 0).astype(F32)
    e_row = lax.broadcasted_iota(jnp.int32, (N_EXPERTS, cols), 0).astype(F32)
    nt_dims = (((1,), (1,)), ((), ()))
    neg = -jnp.inf

    def body(r, carry):
        sl = pl.ds(pl.multiple_of(r * cols, cols), cols)
        h = _norm_mod_rows(x_ref[sl, :], g_ref[...], sc_ref[...], sh_ref[...])
        h_hi = h.astype(BF16)
        h_ref[sl, :] = h_hi
        h_lo = (h - h_hi.astype(F32)).astype(BF16)

        def wt_dot(w_ref, act):
            return lax.dot_general(w_ref[...], act, nt_dims, preferred_element_type=F32)

        lg = wt_dot(whi_ref, h_hi) + (wt_dot(wlo_ref, h_hi) + wt_dot(whi_ref, h_lo)) + b_ref[...]

        g_logit = [lg[N_EXPERTS + g:N_EXPERTS + g + 1, :] for g in range(N_GROUPS)]
        g_max = functools.reduce(jnp.maximum, g_logit)
        g_prob = 1.0 / functools.reduce(lambda a, b: a + b, [jnp.exp(v - g_max) for v in g_logit])
        g_idx = jnp.full_like(g_max, float(N_GROUPS - 1))
        for g in range(N_GROUPS - 2, -1, -1):
            g_idx = jnp.where(g_logit[g] == g_max, float(g), g_idx)

        cand = lg[(N_GROUPS - 1) * EPG:N_GROUPS * EPG, :]
        for g in range(N_GROUPS - 2, -1, -1):
            cand = jnp.where(g_idx == float(g), lg[g * EPG:(g + 1) * EPG, :], cand)
        m0 = jnp.max(cand, axis=0, keepdims=True)
        i0 = jnp.min(jnp.where(cand == m0, sub, float(EPG)), axis=0, keepdims=True)
        cand = jnp.where(sub == i0, neg, cand)
        m1 = jnp.max(cand, axis=0, keepdims=True)
        i1 = jnp.min(jnp.where(cand == m1, sub, float(EPG)), axis=0, keepdims=True)
        t = jnp.exp(m1 - m0)
        w0 = g_prob / (1.0 + t)
        w1 = g_prob * t / (1.0 + t)
        e0 = g_idx * EPG + i0
        e1 = g_idx * EPG + i1

        hit0 = e_row == e0
        hit1 = e_row == e1
        hits = jnp.where(hit0 | hit1, 1.0, 0.0)
        prefix = _dot(hits.astype(BF16), tri)
        before = prefix - hits + cnt_scr[...]
        rank0 = jnp.sum(jnp.where(hit0, before, 0.0), axis=0, keepdims=True)
        rank1 = jnp.sum(jnp.where(hit1, before, 0.0), axis=0, keepdims=True)
        cnt_scr[...] = cnt_scr[...] + prefix[:, cols - 1:cols]
        route_ref[:, sl] = jnp.concatenate([e0, e1, w0, w1, rank0, rank1, jnp.zeros((2, cols), F32)], axis=0)
        return carry

    lax.fori_loop(0, TM // cols, body, 0, unroll=2)


def _router(xs, gain, mod, sc_chunk, sh_chunk, w_hi, w_lo, bias, n_rows):
    return pl.pallas_call(
        _router_kernel,
        out_shape=[jax.ShapeDtypeStruct((n_rows, D), BF16), jax.ShapeDtypeStruct((8, n_rows), F32)],
        grid=(n_rows // TM,),
        in_specs=[
            pl.BlockSpec((TM, D), lambda i: (i, 0)),
            pl.BlockSpec((1, D), lambda i: (0, 0)),
            pl.BlockSpec((None, 1, D), lambda i: (i // (SEQ // TM), 0, sc_chunk)),
            pl.BlockSpec((None, 1, D), lambda i: (i // (SEQ // TM), 0, sh_chunk)),
            pl.BlockSpec((ROUTE_PAD, D), lambda i: (0, 0)),
            pl.BlockSpec((ROUTE_PAD, D), lambda i: (0, 0)),
            pl.BlockSpec((ROUTE_PAD, 1), lambda i: (0, 0)),
        ],
        out_specs=[pl.BlockSpec((TM, D), lambda i: (i, 0)), pl.BlockSpec((8, TM), lambda i: (0, i))],
        scratch_shapes=[pltpu.VMEM((N_EXPERTS, 1), F32)],
        compiler_params=_params("arbitrary"),
        name="moe_router",
    )(xs, gain.reshape(1, D), mod, mod, w_hi, w_lo, bias)


DISPATCH_ROWS = 512
F32_SUBLANES = 8
PERM_ROWS = TOP_K * DISPATCH_ROWS + N_EXPERTS * F32_SUBLANES
RUN_BITS = tuple(1 << b for b in range(9, 2, -1))
PAD_BITS = tuple(1 << b for b in range(7, 2, -1))


def _piece_start(count, size):
    return count & ~(2 * size - 1)


def _dispatch_kernel(cnt_ref, off_ref, dst_ref, zcnt_ref, zdst_ref, h_ref, route_ref, delta_ref, xs_hbm,
                     perm, zbuf, sem, zsem):
    t = pl.program_id(0)

    def aligned(row):
        return pl.multiple_of(row, F32_SUBLANES)

    def pad_copy(e, size):
        z = zcnt_ref[e]
        return z, pltpu.make_async_copy(zbuf.at[pl.ds(0, size), :],
                                        xs_hbm.at[pl.ds(aligned(zdst_ref[e] + _piece_start(z, size)), size), :], zsem)

    @pl.when(t == 0)
    def _():
        zbuf[...] = jnp.zeros_like(zbuf)
        for wait in (False, True):
            def pads(e, carry):
                for size in PAD_BITS:
                    z, cp = pad_copy(e, size)

                    @pl.when((z & size) != 0)
                    def _():
                        cp.wait() if wait else cp.start()
                return carry

            lax.fori_loop(0, N_EXPERTS, pads, 0)

            def tail(k, carry):
                cp = pltpu.make_async_copy(
                    zbuf, xs_hbm.at[pl.ds(aligned(zdst_ref[N_EXPERTS] + k * zbuf.shape[0]), zbuf.shape[0]), :], zsem)
                cp.wait() if wait else cp.start()
                return carry

            lax.fori_loop(0, zcnt_ref[N_EXPERTS], tail, 0)

    e_row = lax.broadcasted_iota(jnp.int32, (N_EXPERTS, DISPATCH_ROWS), 0).astype(F32)
    j_row = lax.broadcasted_iota(jnp.int32, (PERM_ROWS, DISPATCH_ROWS), 0).astype(F32)
    sel = None
    for k in range(TOP_K):
        e_k = route_ref[k:k + 1, :]
        lp = jnp.sum(jnp.where(e_row == e_k, delta_ref[...], 0.0), axis=0, keepdims=True) + route_ref[4 + k:5 + k, :]
        hit = j_row == lp
        sel = hit if sel is None else (sel | hit)
    rows = _dot(jnp.where(sel, 1.0, 0.0).astype(BF16), h_ref[...])

    perm[...] = rows

    for wait in (False, True):
        def runs(e, carry):
            idx = t * N_EXPERTS + e
            c = cnt_ref[idx]
            for size in RUN_BITS:
                start = _piece_start(c, size)
                cp = pltpu.make_async_copy(perm.at[pl.ds(aligned(off_ref[idx] + start), size), :],
                                           xs_hbm.at[pl.ds(aligned(dst_ref[idx] + start), size), :], sem)

                @pl.when((c & size) != 0)
                def _():
                    cp.wait() if wait else cp.start()
            return carry

        lax.fori_loop(0, N_EXPERTS, runs, 0)


def _dispatch(cnt, off, dst, zcnt, zdst, h2, route, delta, n_rows, n_slots):
    n_tiles = n_rows // DISPATCH_ROWS
    grid_spec = pltpu.PrefetchScalarGridSpec(
        num_scalar_prefetch=5,
        grid=(n_tiles,),
        in_specs=[
            pl.BlockSpec((DISPATCH_ROWS, D), lambda t, *_: (t, 0)),
            pl.BlockSpec((8, DISPATCH_ROWS), lambda t, *_: (0, t)),
            pl.BlockSpec((None, N_EXPERTS, 1), lambda t, *_: (t, 0, 0)),
        ],
        out_specs=pl.BlockSpec(memory_space=pl.ANY),
        scratch_shapes=[pltpu.VMEM((PERM_ROWS, D), F32), pltpu.VMEM((MOE_BLOCK // 2, D), F32),
                        pltpu.SemaphoreType.DMA(()), pltpu.SemaphoreType.DMA(())],
    )
    return pl.pallas_call(
        _dispatch_kernel,
        out_shape=jax.ShapeDtypeStruct((n_slots, D), F32),
        grid_spec=grid_spec,
        compiler_params=_params("arbitrary"),
        name="moe_dispatch",
    )(cnt, off, dst, zcnt, zdst, h2, route, delta)


N_CHUNKS = D // 128


def _slab_copy(src_hbm, token, dst, dst_row, sem):
    return pltpu.make_async_copy(src_hbm.at[token], dst.at[:, dst_row, :], sem)


def _start_slab_gather(src_hbm, dst, sem, n, index_of):
    def issue(r, carry):
        _slab_copy(src_hbm, index_of(r), dst, r, sem).start()
        return carry

    lax.fori_loop(0, n, issue, 0, unroll=8)


def _wait_slab_gather(src_hbm, dst, sem, n):
    def drain(r, carry):
        _slab_copy(src_hbm, 0, dst, r, sem).wait()
        return carry

    lax.fori_loop(0, n, drain, 0, unroll=8)


def _expert_kernel(be_ref, nused_ref, x_ref, wgu_ref, wdn_ref, ys_hbm, ybuf0, ybuf1, zbuf, sem_out):
    i = pl.program_id(0)
    n_used = nused_ref[0]
    ybufs = (ybuf0, ybuf1)

    def out_copy(blk, s, c, src=None):
        src = ybufs[s] if src is None else src
        return pltpu.make_async_copy(src.at[c], ys_hbm.at[pl.ds(blk * MOE_BLOCK, MOE_BLOCK), c, :], sem_out.at[s])

    @pl.when(i == 0)
    def _():
        zbuf[...] = jnp.zeros_like(zbuf)

    for s in range(2):
        mine = (i & 1) == s

        @pl.when(mine & (i >= 2) & (i < n_used + 2))
        def _():
            for c in range(N_CHUNKS):
                out_copy(0, s, c).wait()

        @pl.when(mine & (i < n_used))
        def _():
            gu = _dot(x_ref[...].astype(BF16), wgu_ref[...].astype(BF16))
            gate = gu[:, :D_EXPERT]
            act = gate * jax.nn.sigmoid(gate) * gu[:, D_EXPERT:]
            y = _dot(act.astype(BF16), wdn_ref[...].astype(BF16))
            for c in range(N_CHUNKS):
                ybufs[s][c] = y[:, c * 128:(c + 1) * 128]
            for c in range(N_CHUNKS):
                out_copy(i, s, c).start()

        @pl.when(mine & (i >= n_used))
        def _():
            for c in range(N_CHUNKS):
                out_copy(i, s, c, zbuf).start()
            for c in range(N_CHUNKS):
                out_copy(i, s, c, zbuf).wait()


def _experts(block_expert, n_used, xs_sorted, w_gu, w_down, layer, n_blocks):
    grid_spec = pltpu.PrefetchScalarGridSpec(
        num_scalar_prefetch=2,
        grid=(n_blocks,),
        in_specs=[
            pl.BlockSpec((MOE_BLOCK, D), lambda i, be, nu: (jnp.maximum(jnp.minimum(i, nu[0] - 1), 0), 0)),
            pl.BlockSpec((None, None, D, 2 * D_EXPERT), lambda i, be, nu: (layer, be[i], 0, 0)),
            pl.BlockSpec((None, None, D_EXPERT, D), lambda i, be, nu: (layer, be[i], 0, 0)),
        ],
        out_specs=pl.BlockSpec(memory_space=pl.ANY),
        scratch_shapes=[pltpu.VMEM((N_CHUNKS, MOE_BLOCK, 128), F32) for _ in range(3)]
        + [pltpu.SemaphoreType.DMA((2,))],
    )
    return pl.pallas_call(
        _expert_kernel,
        out_shape=jax.ShapeDtypeStruct((n_blocks * MOE_BLOCK, N_CHUNKS, 128), F32),
        grid_spec=grid_spec,
        compiler_params=_params("arbitrary"),
        name="moe_experts",
    )(block_expert, n_used, xs_sorted, w_gu, w_down)


def _combine_kernel(pos_ref, ys_hbm, x_ref, g_ref, w_ref, o_ref, buf, sem):
    i = pl.program_id(0)
    n_tiles = pl.num_programs(0)
    rows = x_ref.shape[0]
    slot = i & 1

    def start(tile, s):
        _start_slab_gather(ys_hbm, buf.at[s], sem.at[s], TOP_K * rows,
                           lambda r: pos_ref[(tile * rows + (r & (rows - 1))) * TOP_K + (r >> int(np.log2(rows)))])

    @pl.when(i == 0)
    def _():
        start(0, 0)

    _wait_slab_gather(ys_hbm, buf.at[slot], sem.at[slot], TOP_K * rows)

    @pl.when(i + 1 < n_tiles)
    def _():
        start(i + 1, 1 - slot)

    w0 = w_ref[:, TOP_K:TOP_K + 1]
    w1 = w_ref[:, TOP_K + 1:TOP_K + 2]
    for c in range(N_CHUNKS):
        lanes = slice(c * 128, (c + 1) * 128)
        mix = w0 * buf[slot, c, 0:rows, :] + w1 * buf[slot, c, rows:2 * rows, :]
        o_ref[:, lanes] = x_ref[:, lanes] + g_ref[:, lanes] * mix


def _combine(pos, ys, xs, mod, gate_chunk, route_cols, n_rows):
    rows = 256
    grid_spec = pltpu.PrefetchScalarGridSpec(
        num_scalar_prefetch=1,
        grid=(n_rows // rows,),
        in_specs=[
            pl.BlockSpec(memory_space=pl.ANY),
            pl.BlockSpec((rows, D), lambda i, pos: (i, 0)),
            pl.BlockSpec((None, 1, D), lambda i, pos: (i // (SEQ // rows), 0, gate_chunk)),
            pl.BlockSpec((rows, 8), lambda i, pos: (i, 0)),
        ],
        out_specs=pl.BlockSpec((rows, D), lambda i, pos: (i, 0)),
        scratch_shapes=[pltpu.VMEM((2, N_CHUNKS, TOP_K * rows, 128), F32), pltpu.SemaphoreType.DMA((2,))],
    )
    return pl.pallas_call(
        _combine_kernel,
        out_shape=jax.ShapeDtypeStruct((n_rows, D), F32),
        grid_spec=grid_spec,
        compiler_params=_params("arbitrary"),
        name="moe_combine",
    )(pos, ys, xs, mod, route_cols)


def _slot_layout(route, n_tok):
    n_assign = n_tok * TOP_K
    n_tiles = n_tok // DISPATCH_ROWS
    expert = route[0:TOP_K, :].T.astype(jnp.int32)
    rank = route[2 * TOP_K:3 * TOP_K, :].T.astype(jnp.int32)
    one_hot = expert[:, :, None] == jnp.arange(N_EXPERTS, dtype=jnp.int32)[None, None, :]
    cnt = jnp.sum(one_hot.reshape(n_tiles, TOP_K * DISPATCH_ROWS, N_EXPERTS).astype(jnp.int32), axis=1)
    cnt8 = (cnt + F32_SUBLANES - 1) // F32_SUBLANES * F32_SUBLANES
    seg = jnp.sum(cnt8, axis=0)
    padded = (seg + MOE_BLOCK - 1) // MOE_BLOCK * MOE_BLOCK
    pad_end = jnp.cumsum(padded)
    pad_start = pad_end - padded
    base = jnp.cumsum(cnt, axis=0) - cnt
    dst = pad_start[None, :] + jnp.cumsum(cnt8, axis=0) - cnt8
    loc_off = jnp.cumsum(cnt8, axis=1) - cnt8
    delta = (loc_off - base).astype(F32).reshape(n_tiles, N_EXPERTS, 1)
    slot_delta = jnp.repeat(dst - base, DISPATCH_ROWS, axis=0)
    pos = (jnp.sum(jnp.where(one_hot, slot_delta[:, None, :], 0), axis=-1) + rank).reshape(n_assign)
    worst_rows = n_assign + (F32_SUBLANES - 1) * N_EXPERTS * n_tiles
    n_blocks = (worst_rows + MOE_BLOCK - 1) // MOE_BLOCK + N_EXPERTS + 2
    block_start = jnp.arange(n_blocks, dtype=jnp.int32) * MOE_BLOCK
    block_expert = jnp.minimum(jnp.sum((pad_end[None, :] <= block_start[:, None]).astype(jnp.int32), axis=1),
                               N_EXPERTS - 1)
    n_used = (pad_end[-1] // MOE_BLOCK).reshape(1)
    tail_chunks = (n_blocks * MOE_BLOCK - pad_end[-1]) // (MOE_BLOCK // 2)
    zero_cnt = jnp.concatenate([padded - seg, tail_chunks.reshape(1)])
    zero_dst = jnp.concatenate([pad_start + seg, pad_end[-1:]])
    tables = (cnt8.reshape(-1), loc_off.reshape(-1), dst.reshape(-1), zero_cnt, zero_dst)
    return tables, delta, block_expert, pos, n_used, n_blocks


def _rope_tables():
    s = jnp.arange(SEQ)
    rows = (s // GRID_W).astype(F32)
    cols = (s % GRID_W).astype(F32)

    def angles(rot_dim):
        n = rot_dim // 4
        inv = ROPE_THETA ** (-jnp.arange(n, dtype=F32) / n)
        return jnp.concatenate([rows[:, None] * inv, cols[:, None] * inv], axis=-1)

    ang_a = angles(HD)
    cos_a, sin_a = jnp.cos(ang_a), jnp.sin(ang_a)
    tab_a = jnp.concatenate([cos_a, cos_a, -sin_a, sin_a], axis=-1)
    ident_a = jnp.concatenate([jnp.ones((T_CTX, HD), F32), jnp.zeros((T_CTX, HD), F32)], axis=-1)
    ang_c = angles(ROPE)
    cos_c, sin_c = jnp.cos(ang_c), jnp.sin(ang_c)
    half = ROPE // 2
    z_half = jnp.zeros((SEQ, half), F32)
    z_pad = jnp.zeros((SEQ, 128 - ROPE), F32)
    tab_c = jnp.concatenate([cos_c, cos_c, z_pad, -sin_c, z_half, z_pad, z_half, sin_c, z_pad], axis=-1)
    ident_c = jnp.concatenate([jnp.ones((T_CTX, ROPE), F32), jnp.zeros((T_CTX, 3 * 128 - ROPE), F32)], axis=-1)
    return jnp.concatenate([tab_a, ident_a], axis=0), jnp.concatenate([tab_c, ident_c], axis=0)


def _pad_head_cols(w, lead):
    w = w.reshape(lead, MLA_H, MLA_QK)
    return jnp.pad(w, ((0, 0), (0, 0), (0, MLA_PAD - MLA_QK))).reshape(lead, MLA_H * MLA_PAD)


def kernel(x, c, ctx, c_ctx, w_mod, b_mod, norm1, norm2, w_in, gqa_q_norm, gqa_k_norm, mla_q_lora_norm, w_uq,
           mla_kv_lora_norm, w_ukv, mla_q_norm, mla_k_norm, conv_w, w_out_a, w_out_b, w_out_c, b_gate, w_o,
           w_group, b_group, w_router, b_router, w_gu, w_down):
    xs = jnp.concatenate([x.reshape(T_LAT, D), ctx.reshape(T_CTX, D)], axis=0)
    cin = jnp.concatenate([c, c_ctx[None, :], jnp.zeros((3, D), F32)], axis=0)
    mod_all = _modulation(cin, w_mod, b_mod)
    rope_a, rope_c = _rope_tables()

    w_in_b = w_in.astype(BF16)
    w_kvq = jnp.concatenate([w_in_b[:, :, :KV_COLS], jnp.zeros((DEPTH, D, 1024 - KV_COLS), BF16),
                             w_in_b[:, :, KV_COLS:REST_OFF]], axis=2)
    w_rest = w_in_b[:, :, REST_OFF:]
    w_a, w_b, w_c, w_o_b = (w.astype(BF16) for w in (w_out_a, w_out_b, w_out_c, w_o))

    for l in range(DEPTH):
        last = l == DEPTH - 1
        n_rows = T_LAT if last else T_ALL
        mod = mod_all[l].reshape(8, 1, 6 * D)

        pkvq = _normed_matmul(xs, norm1[l], mod, 1, 0, w_kvq, l, T_ALL, KVQ_PAD // 2, "in_proj_kvq")
        prest = _normed_matmul(xs, norm1[l], mod, 1, 0, w_rest, l, n_rows, 1536, "in_proj_rest")

        ka, va, kc, vc, qa, qc = _attention_prep(
            pkvq, rope_a, rope_c,
            gqa_q_norm[l].reshape(1, HD), gqa_k_norm[l].reshape(1, HD),
            mla_q_lora_norm[l].reshape(1, Q_LORA), mla_kv_lora_norm[l].reshape(1, KV_LORA),
            jnp.pad(mla_q_norm[l], (0, MLA_PAD - MLA_QK)).reshape(1, MLA_PAD),
            jnp.pad(mla_k_norm[l], (0, MLA_PAD - MLA_QK)).reshape(1, MLA_PAD),
            _pad_head_cols(w_uq[l], Q_LORA).astype(BF16), w_ukv[l].astype(BF16))

        y_a, y_a_ctx = _attention(qa, ka, va, heads=KVH, group=GROUP, dk=HD, dv=HD, tq=ATTN_ROWS // GROUP,
                                  chunk=ATTN_KEY_CHUNK,
                         with_ctx_queries=not last, name="gqa_attention")
        y_c, y_c_ctx = _attention(qc, kc, vc, heads=MLA_H, group=1, dk=MLA_PAD, dv=MLA_V, tq=ATTN_ROWS,
                                  chunk=ATTN_KEY_CHUNK,
                         with_ctx_queries=not last, name="mla_attention")
        y_b = _conv_branch(prest, conv_w[l], n_rows)
        m = _merge(y_a, y_a_ctx, y_b, y_c, y_c_ctx, prest, b_gate[l].reshape(1, 3 * D), w_a, w_b, w_c, l, n_rows)
        xs = _oproj_residual(m, w_o_b, l, xs, mod, 2, n_rows)

        w_r = jnp.concatenate([w_router[l], w_group[l], jnp.zeros((D, ROUTE_PAD - N_GROUPS - N_EXPERTS), F32)], axis=1).T
        w_r_hi = w_r.astype(BF16)
        w_r_lo = (w_r - w_r_hi.astype(F32)).astype(BF16)
        b_r = jnp.concatenate([b_router[l], b_group[l], jnp.zeros((ROUTE_PAD - N_GROUPS - N_EXPERTS,), F32)])
        h2, route = _router(xs, norm2[l], mod, 4, 3, w_r_hi, w_r_lo, b_r.reshape(ROUTE_PAD, 1), n_rows)
        tables, delta, block_expert, pos, n_used, n_blocks = _slot_layout(route, n_rows)
        xs_sorted = _dispatch(*tables, h2, route, delta, n_rows, n_blocks * MOE_BLOCK)
        ys = _experts(block_expert, n_used, xs_sorted, w_gu, w_down, l, n_blocks)
        xs = _combine(pos, ys, xs, mod, 5, route.T, n_rows)

    return xs[:T_LAT].reshape(BATCH, SEQ, D)
```
